```python
import math
import jax
import jax.numpy as jnp
from jax import lax
import numpy as np

D_MODEL = 2048
BATCH = 8
SEQ = 2048
DEPTH = 2
DEC_BATCH = 128
DEC_SEQ = 1
PAST_LEN = 2048
PAGE_SIZE = 128

HEAD_DIM = 128
DSA_H = 4
IDX_H = 16
IDX_DIM = 64
DSA_TOPK = 256
DSA_ROW = 2 * HEAD_DIM + IDX_DIM
MOBA_H = 4
MOBA_BLOCK = 256
MOBA_TOPK = 3
MOBA_QBLOCK = 16
SSM_H = 16
SSM_P = 64
SSM_DI = SSM_H * SSM_P
SSM_G = 2
SSM_N = 128
SSM_CONV = 4
CONV_DIM = SSM_DI + 2 * SSM_G * SSM_N
SSD_CHUNK = 256
DT_MIN = 1e-3
DT_MAX = 1e-1
FOX_H = 4
FOX_ROW = 2 * HEAD_DIM + 1
FOX_GATE_BIAS = 3.0
Q_BLOCK = 128
N_BRANCH = 4
BRANCH_WIDTHS = (DSA_H * HEAD_DIM, MOBA_H * HEAD_DIM, SSM_DI, FOX_H * HEAD_DIM)
IN_WIDTHS = (DSA_H * HEAD_DIM, 2 * HEAD_DIM, IDX_H * IDX_DIM, IDX_DIM, IDX_H,
             3 * MOBA_H * HEAD_DIM,
             SSM_DI, CONV_DIM, SSM_H,
             3 * FOX_H * HEAD_DIM, FOX_H)
IN_WIDTH = sum(IN_WIDTHS)
MEM_LEN = 256
MEM_H = 4
N_EXPERTS = 32
TOP_K = 4
D_FF = 2048
SWIGLU_LIMIT = 7.0
SWIGLU_ALPHA = 1.702
DEEPNORM_ALPHA = (2 * DEPTH) ** 0.25
DEEPNORM_BETA = (8 * DEPTH) ** -0.25
LN_EPS = 1e-5
RMS_EPS = 1e-5

kernel_name = 'hybrid_dsa_moba_ssd_fox_moe_step'


def _split(z, widths):
    out, o = [], 0
    for w in widths:
        out.append(z[..., o:o + w])
        o += w
    return out


def layer_norm(x, g, b):
    xf = x.astype(jnp.float32)
    xc = xf - xf.mean(-1, keepdims=True)
    y = xc * lax.rsqrt((xc * xc).mean(-1, keepdims=True) + LN_EPS)
    return (y * g.astype(jnp.float32) + b.astype(jnp.float32)).astype(x.dtype)


def sweep(fn, qargs, qpos, block):
    n = qpos.shape[0]
    if n <= block or n % block:
        return fn(qargs, qpos)
    nb = n // block
    chunks = tuple(jnp.swapaxes(a.reshape((a.shape[0], nb, block) + a.shape[2:]), 0, 1) for a in qargs)
    out = lax.map(lambda c: fn(c[0], c[1]), (chunks, qpos.reshape(nb, block)))
    out = jnp.swapaxes(out, 0, 1)
    return out.reshape((out.shape[0], n) + out.shape[3:])


def paged_past(pool, layer, page_table):
    rows = pool[layer, page_table]
    return rows.reshape((rows.shape[0], -1) + rows.shape[3:])


def paged_rows(pool, layer, page_table, new_rows, pos):
    past = page_table.shape[1] * PAGE_SIZE
    bidx = jnp.arange(pos.shape[0]).reshape((-1,) + (1,) * (pos.ndim - 1))
    pc = jnp.minimum(pos, past - 1)
    old = pool[layer, page_table[bidx, pc // PAGE_SIZE], pc % PAGE_SIZE]
    new = new_rows[bidx, jnp.clip(pos - past, 0, new_rows.shape[1] - 1)]
    is_past = (pos < past).reshape(pos.shape + (1,) * (old.ndim - pos.ndim))
    return jnp.where(is_past, old, new)


def in_proj(h, lw):
    b, s, _ = h.shape
    (dsa_q, dsa_kv, idx_q, idx_k, idx_w, moba_qkv, ssm_z, ssm_xbc, ssm_dt,
     fox_qkv, fox_f) = _split(h @ lw['w_in'], IN_WIDTHS)
    heads = lambda t, n: t.reshape(b, s, n, -1)
    moba_q, moba_k, moba_v = _split(moba_qkv, (MOBA_H * HEAD_DIM,) * 3)
    fox_q, fox_k, fox_v = _split(fox_qkv, (FOX_H * HEAD_DIM,) * 3)
    log_f = jax.nn.log_sigmoid((fox_f + lw['fox_b_f']).astype(jnp.float32)).astype(h.dtype)
    return {
        'dsa_q': heads(dsa_q, DSA_H),
        'idx_q': heads(idx_q, IDX_H) * IDX_DIM ** -0.5,
        'idx_w': idx_w * IDX_H ** -0.5,
        'dsa_row': jnp.concatenate([dsa_kv, idx_k], axis=-1),
        'moba_q': heads(moba_q, MOBA_H),
        'moba_row': jnp.concatenate([heads(moba_k, MOBA_H), heads(moba_v, MOBA_H)], axis=-1),
        'ssm_z': ssm_z, 'ssm_xbc': ssm_xbc, 'ssm_dt': ssm_dt,
        'fox_q': heads(fox_q, FOX_H),
        'fox_row': jnp.concatenate([heads(fox_k, FOX_H), heads(fox_v, FOX_H), log_f[..., None]], axis=-1),
    }


def dsa_attend(q, iq, iw, kidx, fetch, qpos, n_top):
    n_keys = kidx.shape[1]
    s = jax.nn.relu(jnp.einsum('bqid,bld->bqil', iq, kidx).astype(jnp.float32))
    score = jnp.einsum('bqil,bqi->bql', s, iw.astype(jnp.float32))
    score = jnp.where(jnp.arange(n_keys)[None, :] <= qpos[:, None], score, -jnp.inf)
    top_s, top_i = lax.top_k(score, n_top)
    rows = fetch(top_i)
    logits = jnp.einsum('bqhd,bqkd->bqhk', q, rows[..., :HEAD_DIM]).astype(jnp.float32) * HEAD_DIM ** -0.5
    logits = jnp.where((top_s > -jnp.inf)[:, :, None, :], logits, -jnp.inf)
    pr = jax.nn.softmax(logits, axis=-1).astype(q.dtype)
    return jnp.einsum('bqhk,bqkd->bqhd', pr, rows[..., HEAD_DIM:2 * HEAD_DIM])


def block_means(k):
    b, n_keys, h, d = k.shape
    nb = n_keys // MOBA_BLOCK
    return k[:, :nb * MOBA_BLOCK].astype(jnp.float32).reshape(b, nb, MOBA_BLOCK, h, d).mean(axis=2)


def moba_attend(q, kv, mean_k, qpos, n_sel):
    b, nq, nh, dh = q.shape
    n_keys = kv.shape[1]
    scale = dh ** -0.5
    qblk = qpos // MOBA_BLOCK
    own = qblk[:, None] * MOBA_BLOCK + jnp.arange(MOBA_BLOCK)
    own_kv = kv[:, jnp.minimum(own, n_keys - 1)]
    lo = jnp.einsum('bqhd,bqchd->bqhc', q, own_kv[..., :dh]).astype(jnp.float32) * scale
    logits = [jnp.where((own <= qpos[:, None])[None, :, None, :], lo, -jnp.inf)]
    if n_sel:
        nb = mean_k.shape[1]
        gsc = jnp.einsum('bqhd,bnhd->bqhn', q.astype(jnp.float32), mean_k)
        gsc = jnp.where((jnp.arange(nb)[None, :] < qblk[:, None])[None, :, None, :], gsc, -jnp.inf)
        top_s, top_j = lax.top_k(gsc, n_sel)
        pos = top_j[..., None] * MOBA_BLOCK + jnp.arange(MOBA_BLOCK)
        sel_kv = kv[jnp.arange(b)[:, None, None, None, None], pos,
                    jnp.arange(nh)[None, None, :, None, None]]
        ls = jnp.einsum('bqhd,bqhkcd->bqhkc', q, sel_kv[..., :dh]).astype(jnp.float32) * scale
        ls = jnp.where((top_s > -jnp.inf)[..., None], ls, -jnp.inf)
        logits.append(ls.reshape(b, nq, nh, n_sel * MOBA_BLOCK))
    pr = jax.nn.softmax(jnp.concatenate(logits, axis=-1), axis=-1).astype(q.dtype)
    out = jnp.einsum('bqhc,bqchd->bqhd', pr[..., :MOBA_BLOCK], own_kv[..., dh:])
    if n_sel:
        p_sel = pr[..., MOBA_BLOCK:].reshape(b, nq, nh, n_sel, MOBA_BLOCK)
        out = out + jnp.einsum('bqhkc,bqhkcd->bqhd', p_sel, sel_kv[..., dh:])
    return out


def fox_attend(q, fq, k, v, fk, qpos):
    n_keys = k.shape[1]
    logits = jnp.einsum('bqhd,blhd->bhql', q, k).astype(jnp.float32) * HEAD_DIM ** -0.5
    logits = logits + jnp.swapaxes(fq, 1, 2)[..., :, None] - jnp.swapaxes(fk, 1, 2)[..., None, :]
    logits = jnp.where(jnp.arange(n_keys)[None, :] <= qpos[:, None], logits, -jnp.inf)
    pr = jax.nn.softmax(logits, axis=-1).astype(v.dtype)
    return jnp.einsum('bhql,blhd->bqhd', pr, v)


def segsum(a):
    n = a.shape[-1]
    xx = jnp.broadcast_to(a[..., :, None], a.shape + (n,))
    xx = jnp.where(jnp.tril(jnp.ones((n, n), bool), -1), xx, 0.0)
    out = jnp.cumsum(xx, axis=-2)
    return jnp.where(jnp.tril(jnp.ones((n, n), bool)), out, -jnp.inf)


def ssd_scan(x, dt, a, bm, cm, h0):
    b, s, nh, hp = x.shape
    cs = SSD_CHUNK if s >= SSD_CHUNK else s
    pad = (-s) % cs
    f = lambda t: jnp.pad(t.astype(jnp.float32), [(0, 0), (0, pad)] + [(0, 0)] * (t.ndim - 2))
    x, dt, bm, cm = f(x), f(dt), f(bm), f(cm)
    nc = x.shape[1] // cs
    xdt = (x * dt[..., None]).reshape(b, nc, cs, nh, hp)
    bc = bm.reshape(b, nc, cs, nh, -1)
    cc = cm.reshape(b, nc, cs, nh, -1)
    adt = jnp.transpose((dt * a).reshape(b, nc, cs, nh), (0, 3, 1, 2))
    a_cs = jnp.cumsum(adt, axis=-1)
    scores = jnp.einsum('bclhn,bcshn->bhcls', cc, bc) * jnp.exp(segsum(adt))
    y_diag = jnp.einsum('bhcls,bcshp->bclhp', scores, xdt)
    decay = jnp.exp(a_cs[..., -1:] - a_cs)
    states = jnp.einsum('bclhn,bhcl,bclhp->bchpn', bc, decay, xdt)
    states = jnp.concatenate([h0.astype(jnp.float32)[:, None], states], axis=1)
    chunk_decay = jnp.exp(segsum(jnp.pad(a_cs[..., -1], ((0, 0), (0, 0), (1, 0)))))
    states = jnp.einsum('bhzc,bchpn->bzhpn', chunk_decay, states)
    y_off = jnp.einsum('bclhn,bchpn,bhcl->bclhp', cc, states[:, :-1], jnp.exp(a_cs))
    y = (y_diag + y_off).reshape(b, nc * cs, nh, hp)[:, :s]
    return y, states[:, -1]


def ssm_branch(z_gate, xbc, dt_raw, conv_state, ssm_state, lw):
    b, s, _ = xbc.shape
    xpad = jnp.concatenate([conv_state.astype(xbc.dtype), xbc], axis=1)
    conv = lax.conv_general_dilated(xpad, lw['conv_w'][:, None, :].astype(xbc.dtype), window_strides=(1,),
                                    padding='VALID', dimension_numbers=('NWC', 'WIO', 'NWC'),
                                    feature_group_count=CONV_DIM)
    u = jax.nn.silu(conv + lw['conv_b'])
    xs, bm, cm = _split(u, (SSM_DI, SSM_G * SSM_N, SSM_G * SSM_N))
    rep = SSM_H // SSM_G
    xs = xs.reshape(b, s, SSM_H, SSM_P)
    bm = jnp.repeat(bm.reshape(b, s, SSM_G, SSM_N), rep, axis=2)
    cm = jnp.repeat(cm.reshape(b, s, SSM_G, SSM_N), rep, axis=2)
    dt = jax.nn.softplus((dt_raw + lw['dt_bias']).astype(jnp.float32))
    a = -jnp.exp(lw['a_log'].astype(jnp.float32))
    y, h_new = ssd_scan(xs, dt, a, bm, cm, ssm_state)
    y = y + xs.astype(jnp.float32) * lw['d_skip'].astype(jnp.float32)[:, None]
    y = y.reshape(b, s, SSM_DI) * jax.nn.silu(z_gate.astype(jnp.float32))
    yg = y.reshape(b, s, SSM_G, SSM_DI // SSM_G)
    yg = yg * lax.rsqrt(jnp.mean(yg * yg, axis=-1, keepdims=True) + RMS_EPS)
    y = yg.reshape(b, s, SSM_DI) * lw['ssm_norm_g'].astype(jnp.float32)
    return y.astype(xbc.dtype), h_new.astype(ssm_state.dtype), xpad[:, -(SSM_CONV - 1):]


def merge_branches(h, outs, lw):
    b, s, d = h.shape
    m = 0.0
    projs = (lw['w_br_dsa'], lw['w_br_moba'], lw['w_br_ssm'], lw['w_br_fox'])
    for i in range(N_BRANCH):
        g = jax.nn.sigmoid(h @ lw['w_gate'][:, i * d:(i + 1) * d] + lw['b_gate'][i * d:(i + 1) * d])
        m = m + g * (outs[i].reshape(b, s, -1) @ projs[i])
    return m @ lw['w_o']


def mem_attend(h, mem_kv, lw):
    b, s, _ = h.shape
    q = (h @ lw['w_mq']).reshape(b, s, MEM_H, HEAD_DIM)
    logits = jnp.einsum('bshd,bmhd->bhsm', q, mem_kv[..., :HEAD_DIM]).astype(jnp.float32) * HEAD_DIM ** -0.5
    pr = jax.nn.softmax(logits, axis=-1).astype(h.dtype)
    o = jnp.einsum('bhsm,bmhd->bshd', pr, mem_kv[..., HEAD_DIM:]).reshape(b, s, MEM_H * HEAD_DIM)
    return o @ lw['w_mo']


def clamped_swiglu(hdn):
    g, lin = jnp.split(hdn, 2, axis=-1)
    g = jnp.minimum(g, SWIGLU_LIMIT)
    lin = jnp.clip(lin, -SWIGLU_LIMIT, SWIGLU_LIMIT)
    return g * jax.nn.sigmoid(SWIGLU_ALPHA * g) * (lin + 1.0)


def _moe_block(n_slots):
    blk = 8
    while blk < 128 and blk * 2 * N_EXPERTS <= n_slots:
        blk *= 2
    return blk


def moe(x, lw):
    b, s, d = x.shape
    xt = x.reshape(-1, d)
    n = xt.shape[0]
    logits = (xt @ lw['w_router'] + lw['b_router']).astype(jnp.float32)
    top_v, top_e = lax.top_k(logits, TOP_K)
    gate = jax.nn.softmax(top_v, axis=-1).astype(x.dtype)
    n_slots = n * TOP_K
    blk = _moe_block(n_slots)
    n_blocks = -(-n_slots // blk) + N_EXPERTS
    e_flat = top_e.reshape(-1)
    order = jnp.argsort(e_flat)
    e_sorted = e_flat[order]
    counts = jnp.bincount(e_flat, length=N_EXPERTS)
    padded = (counts + blk - 1) // blk * blk
    pad_end = jnp.cumsum(padded)
    dest = (pad_end - padded)[e_sorted] + jnp.arange(n_slots) - (jnp.cumsum(counts) - counts)[e_sorted]
    tok = jnp.zeros((n_blocks * blk,), jnp.int32).at[dest].set((order // TOP_K).astype(jnp.int32))
    wgt = jnp.zeros((n_blocks * blk,), x.dtype).at[dest].set(gate.reshape(-1)[order])
    blk_e = jnp.minimum(jnp.searchsorted(pad_end, jnp.arange(n_blocks) * blk, side='right'), N_EXPERTS - 1)
    xs = xt[tok].reshape(n_blocks, blk, d)

    def expert_block(args):
        xb, e = args
        hdn = clamped_swiglu(xb @ lw['w1'][e] + lw['b1'][e])
        return hdn @ lw['w2'][e] + lw['b2'][e]

    ys = lax.map(expert_block, (xs, blk_e))
    y = jnp.zeros_like(xt).at[tok].add(ys.reshape(-1, d) * wgt[:, None])
    return y.reshape(b, s, d)


def mixer_prompt(h, lw):
    b, t, _ = h.shape
    p = in_proj(h, lw)
    qpos = jnp.arange(t)
    bidx = jnp.arange(b)[:, None, None]
    dsa_kv = p['dsa_row'][..., :2 * HEAD_DIM]
    kidx = p['dsa_row'][..., 2 * HEAD_DIM:]
    n_top = min(DSA_TOPK, t // 4)
    o_dsa = sweep(lambda a, qp: dsa_attend(a[0], a[1], a[2], kidx, lambda i: dsa_kv[bidx, i], qp, n_top),
                  (p['dsa_q'], p['idx_q'], p['idx_w']), qpos, Q_BLOCK)
    kv = p['moba_row']
    mean_k = block_means(kv[..., :HEAD_DIM])
    n_sel = min(MOBA_TOPK, (t - 1) // MOBA_BLOCK)
    o_moba = sweep(lambda a, qp: moba_attend(a[0], kv, mean_k, qp, n_sel), (p['moba_q'],), qpos, MOBA_QBLOCK)
    o_ssm, ssm_new, conv_new = ssm_branch(p['ssm_z'], p['ssm_xbc'], p['ssm_dt'],
                                          jnp.zeros((b, SSM_CONV - 1, CONV_DIM), h.dtype),
                                          jnp.zeros((b, SSM_H, SSM_P, SSM_N), h.dtype), lw)
    fr = p['fox_row']
    fk, fv = fr[..., :HEAD_DIM], fr[..., HEAD_DIM:2 * HEAD_DIM]
    fcum = jnp.cumsum(fr[..., -1].astype(jnp.float32), axis=1)
    o_fox = sweep(lambda a, qp: fox_attend(a[0], a[1], fk, fv, fcum, qp), (p['fox_q'], fcum), qpos, Q_BLOCK)
    out = merge_branches(h, (o_dsa, o_moba, o_ssm, o_fox), lw)
    return out, (p['dsa_row'], p['moba_row'], p['fox_row'], ssm_new, conv_new)


def mixer_sample(h, lw, layer, cache_dsa, cache_moba, cache_fox, state_ssm, state_conv, page_table):
    b, s, _ = h.shape
    past = page_table.shape[1] * PAGE_SIZE
    n_keys = past + s
    p = in_proj(h, lw)
    qpos = past + jnp.arange(s)
    kidx_past = cache_dsa[layer, page_table, :, 2 * HEAD_DIM:].reshape(b, past, IDX_DIM)
    kidx = jnp.concatenate([kidx_past, p['dsa_row'][..., 2 * HEAD_DIM:]], axis=1)
    fetch = lambda i: paged_rows(cache_dsa, layer, page_table, p['dsa_row'], i)[..., :2 * HEAD_DIM]
    n_top = min(DSA_TOPK, n_keys // 4)
    o_dsa = sweep(lambda a, qp: dsa_attend(a[0], a[1], a[2], kidx, fetch, qp, n_top),
                  (p['dsa_q'], p['idx_q'], p['idx_w']), qpos, Q_BLOCK)
    kv = jnp.concatenate([paged_past(cache_moba, layer, page_table), p['moba_row']], axis=1)
    mean_k = block_means(kv[..., :HEAD_DIM])
    n_sel = min(MOBA_TOPK, (n_keys - 1) // MOBA_BLOCK)
    o_moba = sweep(lambda a, qp: moba_attend(a[0], kv, mean_k, qp, n_sel), (p['moba_q'],), qpos, MOBA_QBLOCK)
    o_ssm, ssm_new, conv_new = ssm_branch(p['ssm_z'], p['ssm_xbc'], p['ssm_dt'],
                                          state_conv[layer], state_ssm[layer], lw)
    fr = jnp.concatenate([paged_past(cache_fox, layer, page_table), p['fox_row']], axis=1)
    fk, fv = fr[..., :HEAD_DIM], fr[..., HEAD_DIM:2 * HEAD_DIM]
    fcum = jnp.cumsum(fr[..., -1].astype(jnp.float32), axis=1)
    o_fox = sweep(lambda a, qp: fox_attend(a[0], a[1], fk, fv, fcum, qp),
                  (p['fox_q'], fcum[:, past:]), qpos, Q_BLOCK)
    out = merge_branches(h, (o_dsa, o_moba, o_ssm, o_fox), lw)
    return out, (p['dsa_row'], p['moba_row'], p['fox_row'], ssm_new, conv_new)


def layer_prompt(x, mem, lw):
    mix, rows = mixer_prompt(x, lw)
    h = layer_norm(DEEPNORM_ALPHA * x + mix, lw['ln1_g'], lw['ln1_b'])
    mem_kv = (mem @ lw['w_mkv']).reshape(mem.shape[0], mem.shape[1], MEM_H, 2 * HEAD_DIM)
    h = layer_norm(DEEPNORM_ALPHA * h + mem_attend(h, mem_kv, lw), lw['ln2_g'], lw['ln2_b'])
    y = layer_norm(DEEPNORM_ALPHA * h + moe(h, lw), lw['ln3_g'], lw['ln3_b'])
    return y, rows + (mem_kv,)


def layer_sample(x, lw, layer, cache_dsa, cache_moba, cache_fox, cache_mem, state_ssm, state_conv, page_table):
    mix, rows = mixer_sample(x, lw, layer, cache_dsa, cache_moba, cache_fox, state_ssm, state_conv, page_table)
    h = layer_norm(DEEPNORM_ALPHA * x + mix, lw['ln1_g'], lw['ln1_b'])
    h = layer_norm(DEEPNORM_ALPHA * h + mem_attend(h, cache_mem[layer], lw), lw['ln2_g'], lw['ln2_b'])
    y = layer_norm(DEEPNORM_ALPHA * h + moe(h, lw), lw['ln3_g'], lw['ln3_b'])
    return y, rows


def setup_inputs(seed: int = 0) -> dict:
    key = jax.random.key(seed)
    keys = iter(jax.random.split(key, 64))

    def nrm(shape, scale):
        return jax.random.normal(next(keys), shape, jnp.float32) * scale

    n_pages = PAST_LEN // PAGE_SIZE
    n_pool = (5 * DEC_BATCH * n_pages + 3) // 4
    d, nl = D_MODEL, DEPTH
    page_table = jax.random.permutation(next(keys), n_pool)[:DEC_BATCH * n_pages]
    page_table = page_table.reshape(DEC_BATCH, n_pages).astype(jnp.int32)
    cache_fox = jnp.concatenate(
        [nrm((nl, n_pool, PAGE_SIZE, FOX_H, 2 * HEAD_DIM), 1.0),
         jax.nn.log_sigmoid(FOX_GATE_BIAS + nrm((nl, n_pool, PAGE_SIZE, FOX_H, 1), 1.0))], axis=-1)
    dt0 = jnp.exp(jax.random.uniform(next(keys), (nl, SSM_H), jnp.float32, math.log(DT_MIN), math.log(DT_MAX)))
    return {
        'x_prompt': nrm((BATCH, SEQ, d), 1.0),
        'x_sample': nrm((DEC_BATCH, DEC_SEQ, d), 1.0),
        'mem_prompt': nrm((BATCH, MEM_LEN, d), 1.0),
        'cache_dsa': nrm((nl, n_pool, PAGE_SIZE, DSA_ROW), 1.0),
        'cache_moba': nrm((nl, n_pool, PAGE_SIZE, MOBA_H, 2 * HEAD_DIM), 1.0),
        'cache_fox': cache_fox,
        'cache_mem': nrm((nl, DEC_BATCH, MEM_LEN, MEM_H, 2 * HEAD_DIM), 1.0),
        'state_ssm': nrm((nl, DEC_BATCH, SSM_H, SSM_P, SSM_N), 0.1),
        'state_conv': nrm((nl, DEC_BATCH, SSM_CONV - 1, CONV_DIM), 1.0),
        'page_table': page_table,
        'w_in': nrm((nl, d, IN_WIDTH), d ** -0.5),
        'fox_b_f': FOX_GATE_BIAS + nrm((nl, FOX_H), 0.1),
        'conv_w': nrm((nl, SSM_CONV, CONV_DIM), SSM_CONV ** -0.5),
        'conv_b': nrm((nl, CONV_DIM), 0.01),
        'dt_bias': dt0 + jnp.log(-jnp.expm1(-dt0)),
        'a_log': jnp.log(jax.random.uniform(next(keys), (nl, SSM_H), jnp.float32, 1.0, 16.0)),
        'd_skip': 1.0 + nrm((nl, SSM_H), 0.01),
        'ssm_norm_g': 1.0 + nrm((nl, SSM_DI), 0.01),
        'w_br_dsa': nrm((nl, BRANCH_WIDTHS[0], d), BRANCH_WIDTHS[0] ** -0.5),
        'w_br_moba': nrm((nl, BRANCH_WIDTHS[1], d), BRANCH_WIDTHS[1] ** -0.5),
        'w_br_ssm': nrm((nl, BRANCH_WIDTHS[2], d), BRANCH_WIDTHS[2] ** -0.5),
        'w_br_fox': nrm((nl, BRANCH_WIDTHS[3], d), BRANCH_WIDTHS[3] ** -0.5),
        'w_gate': nrm((nl, d, N_BRANCH * d), d ** -0.5),
        'b_gate': nrm((nl, N_BRANCH * d), 0.01),
        'w_o': nrm((nl, d, d), DEEPNORM_BETA * d ** -0.5),
        'ln1_g': 1.0 + nrm((nl, d), 0.01),
        'ln1_b': nrm((nl, d), 0.01),
        'w_mq': nrm((nl, d, MEM_H * HEAD_DIM), d ** -0.5),
        'w_mkv': nrm((nl, d, 2 * MEM_H * HEAD_DIM), d ** -0.5),
        'w_mo': nrm((nl, MEM_H * HEAD_DIM, d), DEEPNORM_BETA * (MEM_H * HEAD_DIM) ** -0.5),
        'ln2_g': 1.0 + nrm((nl, d), 0.01),
        'ln2_b': nrm((nl, d), 0.01),
        'w_router': nrm((nl, d, N_EXPERTS), d ** -0.5),
        'b_router': nrm((nl, N_EXPERTS), 0.01),
        'w1': nrm((nl, N_EXPERTS, d, 2 * D_FF), d ** -0.5),
        'b1': nrm((nl, N_EXPERTS, 2 * D_FF), 0.01),
        'w2': nrm((nl, N_EXPERTS, D_FF, d), DEEPNORM_BETA * D_FF ** -0.5),
        'b2': nrm((nl, N_EXPERTS, d), 0.01),
        'ln3_g': 1.0 + nrm((nl, d), 0.01),
        'ln3_b': nrm((nl, d), 0.01),
    }


def reference(x_prompt, x_sample, mem_prompt, cache_dsa, cache_moba, cache_fox, cache_mem, state_ssm,
              state_conv, page_table, w_in, fox_b_f, conv_w, conv_b, dt_bias, a_log, d_skip, ssm_norm_g,
              w_br_dsa, w_br_moba, w_br_ssm, w_br_fox, w_gate, b_gate, w_o, ln1_g, ln1_b,
              w_mq, w_mkv, w_mo, ln2_g, ln2_b, w_router, b_router, w1, b1, w2, b2, ln3_g, ln3_b):
    y_prompt, y_sample = x_prompt, x_sample
    st_p, st_s = [], []
    for l in range(DEPTH):
        lw = dict(w_in=w_in[l], fox_b_f=fox_b_f[l], conv_w=conv_w[l], conv_b=conv_b[l], dt_bias=dt_bias[l],
                  a_log=a_log[l], d_skip=d_skip[l], ssm_norm_g=ssm_norm_g[l], w_br_dsa=w_br_dsa[l],
                  w_br_moba=w_br_moba[l], w_br_ssm=w_br_ssm[l], w_br_fox=w_br_fox[l], w_gate=w_gate[l],
                  b_gate=b_gate[l], w_o=w_o[l], ln1_g=ln1_g[l], ln1_b=ln1_b[l], w_mq=w_mq[l], w_mkv=w_mkv[l],
                  w_mo=w_mo[l], ln2_g=ln2_g[l], ln2_b=ln2_b[l], w_router=w_router[l], b_router=b_router[l],
                  w1=w1[l], b1=b1[l], w2=w2[l], b2=b2[l], ln3_g=ln3_g[l], ln3_b=ln3_b[l])
        y_prompt, rows_p = layer_prompt(y_prompt, mem_prompt, lw)
        y_sample, rows_s = layer_sample(y_sample, lw, l, cache_dsa, cache_moba, cache_fox, cache_mem,
                                        state_ssm, state_conv, page_table)
        st_p.append(rows_p)
        st_s.append(rows_s)
    sp = [jnp.stack(z) for z in zip(*st_p)]
    ss = [jnp.stack(z) for z in zip(*st_s)]
    return (y_prompt, y_sample, sp[0], ss[0], sp[1], ss[1], sp[2], ss[2], sp[3], ss[3], sp[4], ss[4], sp[5])
```

```python
import functools

import jax
import jax.numpy as jnp
from jax import lax
from jax.experimental import pallas as pl
from jax.experimental.pallas import tpu as pltpu

F32 = jnp.float32
I32 = jnp.int32
MXU_DTYPE = jnp.bfloat16
V7X_VMEM_LIMIT = 56 * 1024 * 1024

D_MODEL = 2048
DEPTH = 2
PAGE = 128
HD = 128
DSA_H, IDX_H, IDX_DIM, DSA_TOPK = 4, 16, 64, 256
DSA_ROW = 2 * HD + IDX_DIM
MOBA_H, MOBA_BLOCK, MOBA_TOPK = 4, 256, 3
SSM_H, SSM_P, SSM_G, SSM_N, SSM_CONV = 16, 64, 2, 128, 4
SSM_DI = SSM_H * SSM_P
CONV_DIM = SSM_DI + 2 * SSM_G * SSM_N
SSD_CHUNK = 256
FOX_H = 4
MEM_H = 4
N_EXPERTS, TOP_K, D_FF = 32, 4, 2048
SWIGLU_LIMIT, SWIGLU_ALPHA = 7.0, 1.702
ALPHA = (2 * DEPTH) ** 0.25
LN_EPS = 1e-5
RMS_EPS = 1e-5
IN_WIDTHS = (DSA_H * HD, 2 * HD, IDX_H * IDX_DIM, IDX_DIM, IDX_H, 3 * MOBA_H * HD,
             SSM_DI, CONV_DIM, SSM_H, 3 * FOX_H * HD, FOX_H)
SM_IW, SM_DT, SM_FF = 0, IDX_H, IDX_H + SSM_H
Q_IDX, Q_DSA, Q_MOBA, Q_FOX = 0, 1024, 1536, 2048
Q_WIDTH = 2560
MOE_ROWS = 512
NEG_INF = float("-inf")
INT_MIN = -2 ** 31


def _params(sem):
    return pltpu.CompilerParams(dimension_semantics=sem, vmem_limit_bytes=V7X_VMEM_LIMIT)


def _dot(a, b):
    return jnp.dot(a.astype(MXU_DTYPE), b.astype(MXU_DTYPE), preferred_element_type=F32)


def _dot_nt(a, b):
    return lax.dot_general(a.astype(MXU_DTYPE), b.astype(MXU_DTYPE), (((1,), (1,)), ((), ())),
                           preferred_element_type=F32)


def _dot_exact(a, b):
    return jnp.dot(a, b, preferred_element_type=F32, precision=lax.Precision.HIGHEST)


def _dot_nt_exact(a, b):
    return lax.dot_general(a, b, (((1,), (1,)), ((), ())), preferred_element_type=F32,
                           precision=lax.Precision.HIGHEST)


def _rounded(x):
    return x.astype(MXU_DTYPE).astype(F32)


def _sigmoid(x):
    return 1.0 / (1.0 + jnp.exp(-x))


def _softplus(x):
    return jnp.maximum(x, 0.0) + jnp.log1p(jnp.exp(-jnp.abs(x)))


def _log_sigmoid(x):
    return jnp.minimum(x, 0.0) - jnp.log1p(jnp.exp(-jnp.abs(x)))


def _silu(x):
    return x * _sigmoid(x)


def _pick(n, cands):
    for c in cands:
        if n % c == 0:
            return c
    return n


def _mm_kernel(x_ref, w_ref, o_ref):
    o_ref[...] = _dot(x_ref[...], w_ref[...]).astype(o_ref.dtype)


def mm(x, w, out_dtype=F32):
    m, k = x.shape
    n = w.shape[1]
    tm = _pick(m, (1024, 512, 256, 128))
    tn = _pick(n, (512, 384, 256, 128)) if n % 128 == 0 else n
    return pl.pallas_call(
        _mm_kernel,
        grid=(m // tm, n // tn),
        in_specs=[pl.BlockSpec((tm, k), lambda i, j: (i, 0)),
                  pl.BlockSpec((k, tn), lambda i, j: (0, j))],
        out_specs=pl.BlockSpec((tm, tn), lambda i, j: (i, j)),
        out_shape=jax.ShapeDtypeStruct((m, n), out_dtype),
        compiler_params=_params(("parallel", "arbitrary")),
        name="mm",
    )(x, w)


def _layer_norm(z, g, b):
    mu = jnp.mean(z, axis=-1, keepdims=True)
    zc = z - mu
    var = jnp.mean(zc * zc, axis=-1, keepdims=True)
    return zc * lax.rsqrt(var + LN_EPS) * g + b


def _mm_ln_kernel(a_ref, w_ref, r_ref, g_ref, b_ref, o_ref, ob_ref):
    z = ALPHA * r_ref[...] + _dot(a_ref[...], w_ref[...])
    y = _layer_norm(z, g_ref[...], b_ref[...])
    o_ref[...] = y
    ob_ref[...] = y.astype(ob_ref.dtype)


def mm_ln(a, w, resid, g, b):
    m, k = a.shape
    d = w.shape[1]
    tm = _pick(m, (512, 256, 128))
    return pl.pallas_call(
        _mm_ln_kernel,
        grid=(m // tm,),
        in_specs=[pl.BlockSpec((tm, k), lambda i: (i, 0)),
                  pl.BlockSpec((k, d), lambda i: (0, 0)),
                  pl.BlockSpec((tm, d), lambda i: (i, 0)),
                  pl.BlockSpec((1, d), lambda i: (0, 0)),
                  pl.BlockSpec((1, d), lambda i: (0, 0))],
        out_specs=[pl.BlockSpec((tm, d), lambda i: (i, 0)),
                   pl.BlockSpec((tm, d), lambda i: (i, 0))],
        out_shape=[jax.ShapeDtypeStruct((m, d), F32), jax.ShapeDtypeStruct((m, d), MXU_DTYPE)],
        compiler_params=_params(("parallel",)),
        name="mm_ln",
    )(a, w, resid, g.reshape(1, d), b.reshape(1, d))


def _gate_merge_kernel(x_ref, o0_ref, o1_ref, o2_ref, o3_ref, wg0_ref, wg1_ref, wg2_ref, wg3_ref,
                       bg0_ref, bg1_ref, bg2_ref, bg3_ref, p0_ref, p1_ref, p2_ref, p3_ref, m_ref):
    x = x_ref[...]
    acc = None
    for o_ref, wg_ref, bg_ref, p_ref in ((o0_ref, wg0_ref, bg0_ref, p0_ref), (o1_ref, wg1_ref, bg1_ref, p1_ref),
                                         (o2_ref, wg2_ref, bg2_ref, p2_ref), (o3_ref, wg3_ref, bg3_ref, p3_ref)):
        g = _sigmoid(_dot(x, wg_ref[...]) + bg_ref[...])
        t = g * _dot(o_ref[...], p_ref[...])
        acc = t if acc is None else acc + t
    m_ref[...] = acc.astype(m_ref.dtype)


def gate_merge(xb, outs, w_gate, b_gate, projs):
    m, d = xb.shape
    tm = _pick(m, (1024, 512, 256, 128))
    tn = 256
    nj = d // tn
    in_specs = [pl.BlockSpec((tm, d), lambda i, j: (i, 0))]
    in_specs += [pl.BlockSpec((tm, o.shape[1]), lambda i, j: (i, 0)) for o in outs]
    in_specs += [pl.BlockSpec((d, tn), functools.partial(lambda i, j, br: (0, br * nj + j), br=br)) for br in range(4)]
    in_specs += [pl.BlockSpec((1, tn), functools.partial(lambda i, j, br: (0, br * nj + j), br=br)) for br in range(4)]
    in_specs += [pl.BlockSpec((p.shape[0], tn), lambda i, j: (0, j)) for p in projs]
    bg = b_gate.reshape(1, 4 * d)
    return pl.pallas_call(
        _gate_merge_kernel,
        grid=(m // tm, nj),
        in_specs=in_specs,
        out_specs=pl.BlockSpec((tm, tn), lambda i, j: (i, j)),
        out_shape=jax.ShapeDtypeStruct((m, d), MXU_DTYPE),
        compiler_params=_params(("parallel", "arbitrary")),
        name="gate_merge",
    )(xb, *outs, w_gate, w_gate, w_gate, w_gate, bg, bg, bg, bg, *projs)


def _tri_incl(n):
    row = lax.broadcasted_iota(I32, (n, n), 0)
    col = lax.broadcasted_iota(I32, (n, n), 1)
    return col <= row


def _small_post_kernel(s_ref, bias_ref, post_ref, fcum_ref, carry_ref):
    c = pl.program_id(1)

    @pl.when(c == 0)
    def _():
        carry_ref[...] = jnp.zeros_like(carry_ref)

    raw = s_ref[...]
    z = raw + bias_ref[...]
    lane = lax.broadcasted_iota(I32, z.shape, 1)
    is_f = (lane >= SM_FF) & (lane < SM_FF + FOX_H)
    lf = jnp.where(is_f, _log_sigmoid(z), 0.0)
    post_ref[...] = jnp.where(lane < SM_DT, raw, jnp.where(lane < SM_FF, _softplus(z), lf))
    ck = z.shape[0]
    cs = _dot_exact(jnp.where(_tri_incl(ck), 1.0, 0.0), lf) + carry_ref[...]
    fcum_ref[...] = cs
    carry_ref[...] = cs[ck - 1:ck, :]


def small_post(small, bias_row, nb, t):
    ck = min(t, 256)
    nc = t // ck
    return pl.pallas_call(
        _small_post_kernel,
        grid=(nb, nc),
        in_specs=[pl.BlockSpec((ck, 128), lambda b, c: (b * nc + c, 0)),
                  pl.BlockSpec((1, 128), lambda b, c: (0, 0))],
        out_specs=[pl.BlockSpec((ck, 128), lambda b, c: (b * nc + c, 0)),
                   pl.BlockSpec((ck, 128), lambda b, c: (b * nc + c, 0))],
        out_shape=[jax.ShapeDtypeStruct(small.shape, F32), jax.ShapeDtypeStruct(small.shape, F32)],
        scratch_shapes=[pltpu.VMEM((1, 128), F32)],
        compiler_params=_params(("parallel", "arbitrary")),
        name="small_post",
    )(small, bias_row)


def _softmax_av(lg, v):
    m = jnp.max(lg, axis=-1, keepdims=True)
    p = jnp.exp(lg - m)
    return _dot(p / jnp.sum(p, axis=-1, keepdims=True), v)


def _sort_key(s):
    bits = lax.bitcast_convert_type(s + 0.0, I32)
    return bits ^ (lax.shift_right_arithmetic(bits, 31) & 0x7FFFFFFF)


def _kth_largest_key(count_ge, k, shape):
    kf = float(k)
    t0 = jnp.where(count_ge(jnp.zeros(shape, I32)) >= kf, 0, INT_MIN).astype(I32)

    def body(i, t):
        cand = t | lax.shift_left(jnp.int32(1), 30 - i)
        return jnp.where(count_ge(cand) >= kf, cand, t)

    return lax.fori_loop(0, 31, body, t0)


def _top_rank_mask(gsc, n_sel):
    nb = gsc.shape[1]
    blk = lax.broadcasted_iota(I32, gsc.shape, 1)
    rank = jnp.zeros(gsc.shape, F32)
    for m in range(nb):
        cm = gsc[:, m:m + 1]
        beats = jnp.where(cm > gsc, 1.0, jnp.where(cm == gsc, jnp.where(blk > m, 1.0, 0.0), 0.0))
        rank = rank + beats
    return jnp.where(rank < float(n_sel), jnp.where(gsc > NEG_INF, 1.0, 0.0), 0.0)


def _dsa_prompt_kernel(iq_ref, q_ref, sm_ref, row_ref, o_ref, *, n_top):
    qi = pl.program_id(1)
    tq = q_ref.shape[0]
    t = row_ref.shape[0]
    kidx = row_ref[:, 2 * HD:2 * HD + IDX_DIM].astype(MXU_DTYPE)
    iq = iq_ref[...]
    iw = _rounded(sm_ref[:, SM_IW:SM_IW + IDX_H])
    score = jnp.zeros((tq, t), F32)
    for h in range(IDX_H):
        s = _dot_nt(iq[:, h * IDX_DIM:(h + 1) * IDX_DIM], kidx)
        score = score + _rounded(jnp.maximum(s, 0.0)) * iw[:, h:h + 1]
    tpos = qi * tq + lax.broadcasted_iota(I32, (tq, t), 0)
    lpos = lax.broadcasted_iota(I32, (tq, t), 1)
    causal = lpos <= tpos
    key = jnp.where(causal, _sort_key(score), INT_MIN)
    thr = _kth_largest_key(lambda c: jnp.sum(jnp.where(key >= c, 1.0, 0.0), axis=-1, keepdims=True),
                           n_top, (tq, 1))
    sel = key >= thr
    kb = row_ref[:, 0:HD].astype(MXU_DTYPE)
    vb = row_ref[:, HD:2 * HD].astype(MXU_DTYPE)
    for h in range(DSA_H):
        lg = _dot_nt(q_ref[:, h * HD:(h + 1) * HD], kb) * HD ** -0.5
        lg = jnp.where(causal, jnp.where(sel, lg, NEG_INF), NEG_INF)
        o_ref[:, h * HD:(h + 1) * HD] = _softmax_av(lg, vb).astype(o_ref.dtype)


def dsa_prompt(q_all, post, dsa_row, nb, t, n_top):
    tq = min(t, 256)
    nq = t // tq
    return pl.pallas_call(
        functools.partial(_dsa_prompt_kernel, n_top=n_top),
        grid=(nb, nq),
        in_specs=[pl.BlockSpec((tq, IDX_H * IDX_DIM), lambda b, i: (b * nq + i, Q_IDX // 1024)),
                  pl.BlockSpec((tq, DSA_H * HD), lambda b, i: (b * nq + i, Q_DSA // 512)),
                  pl.BlockSpec((tq, 128), lambda b, i: (b * nq + i, 0)),
                  pl.BlockSpec((t, DSA_ROW), lambda b, i: (b, 0))],
        out_specs=pl.BlockSpec((tq, DSA_H * HD), lambda b, i: (b * nq + i, 0)),
        out_shape=jax.ShapeDtypeStruct((nb * t, DSA_H * HD), MXU_DTYPE),
        compiler_params=_params(("parallel", "arbitrary")),
        name="dsa_prompt",
    )(q_all, q_all, post, dsa_row)


def _moba_prompt_kernel(q_ref, kv_ref, o_ref, kb_ref, vb_ref, mean_ref, *, n_sel):
    qi = pl.program_id(2)
    t = kv_ref.shape[0]
    nblk = t // MOBA_BLOCK

    @pl.when(qi == 0)
    def _():
        kb_ref[...] = kv_ref[:, 0:HD].astype(MXU_DTYPE)
        vb_ref[...] = kv_ref[:, HD:2 * HD].astype(MXU_DTYPE)
        for n in range(nblk):
            mean_ref[n:n + 1, :] = jnp.mean(kv_ref[n * MOBA_BLOCK:(n + 1) * MOBA_BLOCK, 0:HD], axis=0,
                                            keepdims=True)

    q = q_ref[...]
    logits = _dot_nt(q, kb_ref[...]) * HD ** -0.5
    tq = q.shape[0]
    own_causal = (lax.broadcasted_iota(I32, (tq, MOBA_BLOCK), 1) <= lax.broadcasted_iota(I32, (tq, MOBA_BLOCK), 0))
    if n_sel:
        gsc = _dot_nt(q, mean_ref[...])
        gsc = jnp.where(lax.broadcasted_iota(I32, gsc.shape, 1) < qi, gsc, NEG_INF)
        selm = _top_rank_mask(gsc, n_sel)
    pieces = []
    for n in range(nblk):
        lg = logits[:, n * MOBA_BLOCK:(n + 1) * MOBA_BLOCK]
        piece = jnp.where(qi == n, jnp.where(own_causal, lg, NEG_INF), NEG_INF)
        if n_sel:
            piece = jnp.where(selm[:, n:n + 1] > 0.5, lg, piece)
        pieces.append(piece)
    lg = jnp.concatenate(pieces, axis=1) if nblk > 1 else pieces[0]
    o_ref[...] = _softmax_av(lg, vb_ref[...]).astype(o_ref.dtype)


def moba_prompt(q_all, moba_row, nb, t, n_sel):
    nq = t // MOBA_BLOCK
    return pl.pallas_call(
        functools.partial(_moba_prompt_kernel, n_sel=n_sel),
        grid=(nb, MOBA_H, nq),
        in_specs=[pl.BlockSpec((MOBA_BLOCK, HD), lambda b, h, i: (b * nq + i, Q_MOBA // HD + h)),
                  pl.BlockSpec((t, 2 * HD), lambda b, h, i: (b, h))],
        out_specs=pl.BlockSpec((MOBA_BLOCK, HD), lambda b, h, i: (b * nq + i, h)),
        out_shape=jax.ShapeDtypeStruct((nb * t, MOBA_H * HD), MXU_DTYPE),
        scratch_shapes=[pltpu.VMEM((t, HD), MXU_DTYPE), pltpu.VMEM((t, HD), MXU_DTYPE),
                        pltpu.VMEM((nq, HD), F32)],
        compiler_params=_params(("parallel", "parallel", "arbitrary")),
        name="moba_prompt",
    )(q_all, moba_row)


def _fox_prompt_kernel(q_ref, kv_ref, fc_ref, fk_ref, o_ref, kb_ref, vb_ref):
    h = pl.program_id(1)
    qi = pl.program_id(2)

    @pl.when(qi == 0)
    def _():
        kb_ref[...] = kv_ref[:, 0:HD].astype(MXU_DTYPE)
        vb_ref[...] = kv_ref[:, HD:2 * HD].astype(MXU_DTYPE)

    tq = q_ref.shape[0]
    t = kv_ref.shape[0]
    fc = fc_ref[...]
    fq = jnp.sum(jnp.where(lax.broadcasted_iota(I32, fc.shape, 1) == SM_FF + h, fc, 0.0), axis=-1, keepdims=True)
    lg = _dot_nt(q_ref[...], kb_ref[...]) * HD ** -0.5
    lg = lg + fq - fk_ref[...]
    causal = lax.broadcasted_iota(I32, (tq, t), 1) <= qi * tq + lax.broadcasted_iota(I32, (tq, t), 0)
    lg = jnp.where(causal, lg, NEG_INF)
    o_ref[...] = _softmax_av(lg, vb_ref[...]).astype(o_ref.dtype)


def fox_prompt(q_all, fox_kv, fcum, fk_rows, nb, t):
    tq = min(t, 256)
    nq = t // tq
    return pl.pallas_call(
        _fox_prompt_kernel,
        grid=(nb, FOX_H, nq),
        in_specs=[pl.BlockSpec((tq, HD), lambda b, h, i: (b * nq + i, Q_FOX // HD + h)),
                  pl.BlockSpec((t, 2 * HD), lambda b, h, i: (b, h)),
                  pl.BlockSpec((tq, 128), lambda b, h, i: (b * nq + i, 0)),
                  pl.BlockSpec((None, 1, t), lambda b, h, i: (b * FOX_H + h, 0, 0))],
        out_specs=pl.BlockSpec((tq, HD), lambda b, h, i: (b * nq + i, h)),
        out_shape=jax.ShapeDtypeStruct((nb * t, FOX_H * HD), MXU_DTYPE),
        scratch_shapes=[pltpu.VMEM((t, HD), MXU_DTYPE), pltpu.VMEM((t, HD), MXU_DTYPE)],
        compiler_params=_params(("parallel", "parallel", "arbitrary")),
        name="fox_prompt",
    )(q_all, fox_kv, fcum, fk_rows)


def _gated_group_norm(y, z, ng):
    y = y * _silu(z)
    gw = SSM_DI // SSM_G
    parts = []
    for g in range(SSM_G):
        yg = y[:, g * gw:(g + 1) * gw]
        parts.append(yg * lax.rsqrt(jnp.mean(yg * yg, axis=-1, keepdims=True) + RMS_EPS))
    return jnp.concatenate(parts, axis=1) * ng


def _chunk_decays(alast_ref, z, nc):
    rows = []
    for j in range(nc):
        e = jnp.zeros((1, 128), F32)
        for m in range(j + 1, nc):
            e = e + jnp.where(m < z, alast_ref[m], 0.0)
        rows.append(jnp.where(j < z, _rounded(jnp.exp(e)), 0.0))
    return rows


def _carried_state(dec, sloc_ref, h):
    acc = None
    for j, d in enumerate(dec):
        t = d[:, SM_DT + h:SM_DT + h + 1] * sloc_ref[j, h].astype(F32)
        acc = t if acc is None else acc + t
    return acc


def _ssd_prompt_kernel(xbc_ref, z_ref, post_ref, cw_ref, cb_ref, hp_ref, ng_ref, o_ref, st_ref,
                       sloc_ref, alast_ref, tail_ref):
    c = pl.program_id(1)
    nc = sloc_ref.shape[0]
    ln = xbc_ref.shape[0]

    @pl.when(c == 0)
    def _():
        sloc_ref[...] = jnp.zeros_like(sloc_ref)
        alast_ref[...] = jnp.zeros_like(alast_ref)
        tail_ref[...] = jnp.zeros_like(tail_ref)

    xin = xbc_ref[...]
    xx = _rounded(jnp.concatenate([tail_ref[...], xin], axis=0))
    cw = _rounded(cw_ref[...])
    conv = cw[0:1, :] * xx[5:5 + ln, :]
    for j in range(1, SSM_CONV):
        conv = conv + cw[j:j + 1, :] * xx[5 + j:5 + j + ln, :]
    conv = conv + cb_ref[...]
    tail_ref[...] = xin[ln - 8:ln, :]
    u = _silu(conv)
    xs = u[:, :SSM_DI]
    xs_t = xs.T
    lane = lax.broadcasted_iota(I32, (ln, 128), 1)
    is_dt = (lane >= SM_DT) & (lane < SM_DT + SSM_H)
    dt = jnp.where(is_dt, post_ref[...], 0.0)
    a_row = -jnp.exp(hp_ref[1:2, :])
    adt = jnp.where(is_dt, dt * a_row, 0.0)
    low = _tri_incl(ln)
    acs = _dot_exact(jnp.where(low, 1.0, 0.0), adt)
    acs_t = acs.T
    dt_t = dt.T
    a_last = acs[ln - 1:ln, :]
    dec = _chunk_decays(alast_ref, c, nc)
    ys = []
    for g in range(SSM_G):
        bg = u[:, SSM_DI + g * SSM_N:SSM_DI + (g + 1) * SSM_N]
        cg = u[:, SSM_DI + SSM_G * SSM_N + g * SSM_N:SSM_DI + SSM_G * SSM_N + (g + 1) * SSM_N]
        cb = _dot_nt(cg, bg)
        for hh in range(SSM_H // SSM_G):
            h = g * (SSM_H // SSM_G) + hh
            col = acs[:, SM_DT + h:SM_DT + h + 1]
            row = acs_t[SM_DT + h:SM_DT + h + 1, :]
            last = a_last[:, SM_DT + h:SM_DT + h + 1]
            x_h = xs[:, h * SSM_P:(h + 1) * SSM_P]
            xdt = x_h * dt[:, SM_DT + h:SM_DT + h + 1]
            scores = cb * jnp.where(low, jnp.exp(col - row), 0.0)
            y = _dot(scores, xdt) + jnp.exp(col) * _dot_nt(cg, _carried_state(dec, sloc_ref, h))
            ys.append(y + x_h * hp_ref[2:3, SM_DT + h:SM_DT + h + 1])
            xdt_t = (xs_t[h * SSM_P:(h + 1) * SSM_P, :] * dt_t[SM_DT + h:SM_DT + h + 1, :]) * jnp.exp(last - row)
            sloc_ref[c, h] = _dot(xdt_t, bg).astype(sloc_ref.dtype)
    alast_ref[c] = a_last
    y = jnp.concatenate(ys, axis=1)
    o_ref[...] = _gated_group_norm(y, z_ref[...], ng_ref[...]).astype(o_ref.dtype)

    @pl.when(c == nc - 1)
    def _():
        dec_end = _chunk_decays(alast_ref, nc, nc)
        for h in range(SSM_H):
            st_ref[h] = _carried_state(dec_end, sloc_ref, h)


def ssd_prompt(xbc, z, post, conv_w, conv_b, head_params, norm_g, nb, t):
    ck = min(t, SSD_CHUNK)
    nc = t // ck
    return pl.pallas_call(
        _ssd_prompt_kernel,
        grid=(nb, nc),
        in_specs=[pl.BlockSpec((ck, CONV_DIM), lambda b, c: (b * nc + c, 0)),
                  pl.BlockSpec((ck, SSM_DI), lambda b, c: (b * nc + c, 0)),
                  pl.BlockSpec((ck, 128), lambda b, c: (b * nc + c, 0)),
                  pl.BlockSpec((SSM_CONV, CONV_DIM), lambda b, c: (0, 0)),
                  pl.BlockSpec((1, CONV_DIM), lambda b, c: (0, 0)),
                  pl.BlockSpec((8, 128), lambda b, c: (0, 0)),
                  pl.BlockSpec((1, SSM_DI), lambda b, c: (0, 0))],
        out_specs=[pl.BlockSpec((ck, SSM_DI), lambda b, c: (b * nc + c, 0)),
                   pl.BlockSpec((None, SSM_H, SSM_P, SSM_N), lambda b, c: (b, 0, 0, 0))],
        out_shape=[jax.ShapeDtypeStruct((nb * t, SSM_DI), MXU_DTYPE),
                   jax.ShapeDtypeStruct((nb, SSM_H, SSM_P, SSM_N), F32)],
        scratch_shapes=[pltpu.VMEM((nc, SSM_H, SSM_P, SSM_N), MXU_DTYPE), pltpu.VMEM((nc, 1, 128), F32),
                        pltpu.VMEM((8, CONV_DIM), F32)],
        compiler_params=_params(("parallel", "arbitrary")),
        name="ssd_prompt",
    )(xbc, z, post, conv_w, conv_b.reshape(1, CONV_DIM), head_params, norm_g.reshape(1, SSM_DI))


def _mem_prompt_kernel(q_ref, kv_ref, o_ref):
    for h in range(MEM_H):
        k = kv_ref[:, h * 2 * HD:h * 2 * HD + HD]
        v = kv_ref[:, h * 2 * HD + HD:(h + 1) * 2 * HD]
        lg = _dot_nt(q_ref[:, h * HD:(h + 1) * HD], k) * HD ** -0.5
        o_ref[:, h * HD:(h + 1) * HD] = _softmax_av(lg, v).astype(o_ref.dtype)


def mem_prompt_attn(mq, mem_kv, nb, t, mem_len):
    tq = _pick(t, (512, 256, 128))
    nq = t // tq
    return pl.pallas_call(
        _mem_prompt_kernel,
        grid=(nb, nq),
        in_specs=[pl.BlockSpec((tq, MEM_H * HD), lambda b, i: (b * nq + i, 0)),
                  pl.BlockSpec((mem_len, MEM_H * 2 * HD), lambda b, i: (b, 0))],
        out_specs=pl.BlockSpec((tq, MEM_H * HD), lambda b, i: (b * nq + i, 0)),
        out_shape=jax.ShapeDtypeStruct((nb * t, MEM_H * HD), MXU_DTYPE),
        compiler_params=_params(("parallel", "arbitrary")),
        name="mem_prompt",
    )(mq, mem_kv)


def _split_cols(w, widths):
    out, o = [], 0
    for wd in widths:
        out.append(w[:, o:o + wd])
        o += wd
    return out


def _interleave_heads(k, v, n_heads):
    pieces = []
    for h in range(n_heads):
        pieces += [k[:, h * HD:(h + 1) * HD], v[:, h * HD:(h + 1) * HD]]
    return jnp.concatenate(pieces, axis=1)


def prep_layer(l, p):
    c = lambda a: a.astype(MXU_DTYPE)
    (dsa_q, dsa_kv, idx_q, idx_k, idx_w, moba_qkv, ssm_z, ssm_xbc, ssm_dt, fox_qkv, fox_f) = _split_cols(
        p["w_in"][l], IN_WIDTHS)
    moba_q, moba_k, moba_v = _split_cols(moba_qkv, (MOBA_H * HD,) * 3)
    fox_q, fox_k, fox_v = _split_cols(fox_qkv, (FOX_H * HD,) * 3)
    pad = jnp.zeros((D_MODEL, 128 - SM_FF - FOX_H), F32)
    small_bias = jnp.zeros((1, 128), F32)
    small_bias = small_bias.at[0, SM_DT:SM_DT + SSM_H].set(p["dt_bias"][l])
    small_bias = small_bias.at[0, SM_FF:SM_FF + FOX_H].set(p["fox_b_f"][l])
    head_params = jnp.zeros((8, 128), F32)
    head_params = head_params.at[1, SM_DT:SM_DT + SSM_H].set(p["a_log"][l])
    head_params = head_params.at[2, SM_DT:SM_DT + SSM_H].set(p["d_skip"][l])
    router_pad = jnp.zeros((D_MODEL, 128 - N_EXPERTS), F32)
    return dict(
        w_q=c(jnp.concatenate([idx_q * IDX_DIM ** -0.5, dsa_q, moba_q, fox_q], axis=1)),
        w_dsa_row=c(jnp.concatenate([dsa_kv, idx_k], axis=1)),
        w_moba_row=c(_interleave_heads(moba_k, moba_v, MOBA_H)),
        w_fox_kv=c(_interleave_heads(fox_k, fox_v, FOX_H)),
        w_z=c(ssm_z), w_xbc=c(ssm_xbc),
        w_small=c(jnp.concatenate([idx_w * IDX_H ** -0.5, ssm_dt, fox_f, pad], axis=1)),
        small_bias=small_bias, head_params=head_params,
        conv_w=p["conv_w"][l], conv_b=p["conv_b"][l], norm_g=p["ssm_norm_g"][l],
        projs=[c(p["w_br_dsa"][l]), c(p["w_br_moba"][l]), c(p["w_br_ssm"][l]), c(p["w_br_fox"][l])],
        w_gate=c(p["w_gate"][l]), b_gate=p["b_gate"][l], w_o=c(p["w_o"][l]),
        ln1=(p["ln1_g"][l], p["ln1_b"][l]), ln2=(p["ln2_g"][l], p["ln2_b"][l]), ln3=(p["ln3_g"][l], p["ln3_b"][l]),
        w_mq=c(p["w_mq"][l]), w_mkv=c(p["w_mkv"][l]), w_mo=c(p["w_mo"][l]),
        w_router=c(jnp.concatenate([p["w_router"][l], router_pad], axis=1)),
        b_router=jnp.concatenate([p["b_router"][l], jnp.full((128 - N_EXPERTS,), NEG_INF, F32)]).reshape(1, 128),
        w1=p["w1"][l], b1=p["b1"][l], w2=p["w2"][l], b2=p["b2"][l],
    )


def in_proj(xb, lw, q_dtype=MXU_DTYPE):
    return dict(
        q_all=mm(xb, lw["w_q"], q_dtype),
        dsa_row=mm(xb, lw["w_dsa_row"]),
        moba_row=mm(xb, lw["w_moba_row"]),
        fox_kv=mm(xb, lw["w_fox_kv"]),
        z=mm(xb, lw["w_z"]),
        xbc=mm(xb, lw["w_xbc"]),
        small=mm(xb, lw["w_small"]),
    )


def mixer_prompt(xb, lw, nb, t):
    pr = in_proj(xb, lw)
    post, fcum = small_post(pr["small"], lw["small_bias"], nb, t)
    o_dsa = dsa_prompt(pr["q_all"], post, pr["dsa_row"], nb, t, min(DSA_TOPK, t // 4))
    o_moba = moba_prompt(pr["q_all"], pr["moba_row"], nb, t, min(MOBA_TOPK, (t - 1) // MOBA_BLOCK))
    o_ssm, ssm_new = ssd_prompt(pr["xbc"], pr["z"], post, lw["conv_w"], lw["conv_b"], lw["head_params"],
                                lw["norm_g"], nb, t)
    fk_rows = fcum[:, SM_FF:SM_FF + FOX_H].reshape(nb, t, FOX_H).transpose(0, 2, 1).reshape(nb * FOX_H, 1, t)
    o_fox = fox_prompt(pr["q_all"], pr["fox_kv"], fcum, fk_rows, nb, t)
    logf = post[:, SM_FF:SM_FF + FOX_H].reshape(nb, t, FOX_H, 1)
    rows = (pr["dsa_row"].reshape(nb, t, DSA_ROW),
            pr["moba_row"].reshape(nb, t, MOBA_H, 2 * HD),
            jnp.concatenate([pr["fox_kv"].reshape(nb, t, FOX_H, 2 * HD), logf], axis=-1),
            ssm_new,
            pr["xbc"].reshape(nb, t, CONV_DIM)[:, t - (SSM_CONV - 1):])
    return (o_dsa, o_moba, o_ssm, o_fox), rows


MOE_TOK = 128


def _router_kernel(h_ref, w_ref, b_ref, idx_ref, gate_ref, cnt_ref, carry_ref):
    i = pl.program_id(0)

    @pl.when(i == 0)
    def _():
        carry_ref[...] = jnp.zeros_like(carry_ref)

    work = _dot(h_ref[...], w_ref[...]) + b_ref[...]
    tm = work.shape[0]
    lane = lax.broadcasted_iota(I32, work.shape, 1)
    hot = jnp.zeros(work.shape, F32)
    vals, idxs = [], []
    for _k in range(TOP_K):
        m = jnp.max(work, axis=-1, keepdims=True)
        idx = jnp.min(jnp.where(work == m, lane, 128), axis=-1, keepdims=True)
        vals.append(m)
        idxs.append(idx)
        hot = hot + jnp.where(lane == idx, 1.0, 0.0)
        work = jnp.where(lane == idx, NEG_INF, work)
    es = [jnp.exp(v - vals[0]) for v in vals]
    tot = es[0] + es[1] + es[2] + es[3]
    strict = lax.broadcasted_iota(I32, (tm, tm), 1) < lax.broadcasted_iota(I32, (tm, tm), 0)
    prefix = _dot(jnp.where(strict, 1.0, 0.0), hot) + carry_ref[...]
    out_i = jnp.zeros(work.shape, I32)
    out_g = jnp.zeros(work.shape, F32)
    for k in range(TOP_K):
        rank = jnp.sum(jnp.where(lane == idxs[k], prefix, 0.0), axis=-1, keepdims=True)
        out_i = jnp.where(lane == k, idxs[k], out_i)
        out_i = jnp.where(lane == TOP_K + k, rank.astype(I32), out_i)
        out_g = jnp.where(lane == k, es[k] / tot, out_g)
    idx_ref[...] = out_i
    gate_ref[...] = out_g
    carry_ref[...] = carry_ref[...] + jnp.sum(hot, axis=0, keepdims=True)
    cnt_ref[...] = carry_ref[...]


def router(hb, w_router, b_router):
    n, d = hb.shape
    tm = MOE_TOK
    return pl.pallas_call(
        _router_kernel,
        grid=(n // tm,),
        in_specs=[pl.BlockSpec((tm, d), lambda i: (i, 0)),
                  pl.BlockSpec((d, 128), lambda i: (0, 0)),
                  pl.BlockSpec((1, 128), lambda i: (0, 0))],
        out_specs=[pl.BlockSpec((tm, 128), lambda i: (i, 0)),
                   pl.BlockSpec((tm, 128), lambda i: (i, 0)),
                   pl.BlockSpec((1, 128), lambda i: (0, 0))],
        out_shape=[jax.ShapeDtypeStruct((n, 128), I32), jax.ShapeDtypeStruct((n, 128), F32),
                   jax.ShapeDtypeStruct((1, 128), F32)],
        scratch_shapes=[pltpu.VMEM((1, 128), F32)],
        compiler_params=_params(("arbitrary",)),
        name="router",
    )(hb, w_router, b_router)


def _row_copy(src, dst, sem):
    return pltpu.make_async_copy(src, dst, sem)


def _dispatch_kernel(dest_ref, x_ref, xs_in_ref, xs_ref, sem):
    del xs_in_ref
    n = x_ref.shape[0]

    def issue(r, carry):
        for k in range(TOP_K):
            d = dest_ref[0, r * TOP_K + k]
            _row_copy(x_ref.at[pl.ds(r, 1), :], xs_ref.at[pl.ds(d, 1), :], sem).start()
        return carry

    lax.fori_loop(0, n, issue, 0)

    def drain(r, carry):
        for _k in range(TOP_K):
            _row_copy(x_ref.at[pl.ds(0, 1), :], xs_ref.at[pl.ds(0, 1), :], sem).wait()
        return carry

    lax.fori_loop(0, n, drain, 0)


def dispatch(dest3, x, n_rows):
    n, d = x.shape
    xs0 = jnp.zeros((n_rows, d), x.dtype)
    return pl.pallas_call(
        _dispatch_kernel,
        grid=(n // MOE_TOK,),
        in_specs=[pl.BlockSpec((None, 1, MOE_TOK * TOP_K), lambda i: (i, 0, 0), memory_space=pltpu.SMEM),
                  pl.BlockSpec((MOE_TOK, d), lambda i: (i, 0)),
                  pl.BlockSpec(memory_space=pl.ANY)],
        out_specs=pl.BlockSpec(memory_space=pl.ANY),
        out_shape=jax.ShapeDtypeStruct((n_rows, d), x.dtype),
        scratch_shapes=[pltpu.SemaphoreType.DMA(())],
        input_output_aliases={2: 0},
        compiler_params=_params(("arbitrary",)),
        name="moe_dispatch",
    )(dest3, x, xs0)


def _gmm1_kernel(be_ref, bf_ref, xs_ref, wg_ref, wl_ref, bg_ref, bl_ref, o_ref, wgb_ref, wlb_ref):
    del be_ref
    i = pl.program_id(1)
    flags = bf_ref[i]

    @pl.when((flags & 2) != 0)
    def _():
        wgb_ref[...] = wg_ref[...].astype(wgb_ref.dtype)
        wlb_ref[...] = wl_ref[...].astype(wlb_ref.dtype)

    @pl.when((flags & 1) != 0)
    def _():
        x = xs_ref[...]
        g = jnp.minimum(_dot(x, wgb_ref[...]) + bg_ref[...], SWIGLU_LIMIT)
        lin = jnp.clip(_dot(x, wlb_ref[...]) + bl_ref[...], -SWIGLU_LIMIT, SWIGLU_LIMIT)
        o_ref[...] = (g * _sigmoid(SWIGLU_ALPHA * g) * (lin + 1.0)).astype(o_ref.dtype)

    @pl.when((flags & 1) == 0)
    def _():
        o_ref[...] = jnp.zeros_like(o_ref)


def gmm1(blk_e, blk_flags, xs, w1, b1):
    s, d = xs.shape
    nblk = s // MOE_ROWS
    tn = 512
    nj = D_FF // tn
    gs = pltpu.PrefetchScalarGridSpec(
        num_scalar_prefetch=2, grid=(nj, nblk),
        in_specs=[pl.BlockSpec((MOE_ROWS, d), lambda j, i, be, bf: (i, 0)),
                  pl.BlockSpec((None, d, tn), lambda j, i, be, bf: (be[i], 0, j)),
                  pl.BlockSpec((None, d, tn), lambda j, i, be, bf: (be[i], 0, nj + j)),
                  pl.BlockSpec((None, 1, tn), lambda j, i, be, bf: (be[i], 0, j)),
                  pl.BlockSpec((None, 1, tn), lambda j, i, be, bf: (be[i], 0, nj + j))],
        out_specs=pl.BlockSpec((MOE_ROWS, tn), lambda j, i, be, bf: (i, j)),
        scratch_shapes=[pltpu.VMEM((d, tn), MXU_DTYPE), pltpu.VMEM((d, tn), MXU_DTYPE)])
    b1r = b1.reshape(N_EXPERTS, 1, 2 * D_FF)
    return pl.pallas_call(
        _gmm1_kernel, grid_spec=gs,
        out_shape=jax.ShapeDtypeStruct((s, D_FF), MXU_DTYPE),
        compiler_params=_params(("arbitrary", "arbitrary")),
        name="moe_gmm1",
    )(blk_e, blk_flags, xs, w1, w1, b1r, b1r)


def _gmm2_kernel(be_ref, bf_ref, h_ref, w_ref, b_ref, o_ref, wb_ref):
    del be_ref
    i = pl.program_id(1)
    flags = bf_ref[i]

    @pl.when((flags & 2) != 0)
    def _():
        wb_ref[...] = w_ref[...].astype(wb_ref.dtype)

    @pl.when((flags & 1) != 0)
    def _():
        o_ref[...] = _dot(h_ref[...], wb_ref[...]) + b_ref[...]

    @pl.when((flags & 1) == 0)
    def _():
        o_ref[...] = jnp.zeros_like(o_ref)


def gmm2(blk_e, blk_flags, hs, w2, b2):
    s, f = hs.shape
    d = w2.shape[2]
    nblk = s // MOE_ROWS
    tn = 512
    nj = d // tn
    gs = pltpu.PrefetchScalarGridSpec(
        num_scalar_prefetch=2, grid=(nj, nblk),
        in_specs=[pl.BlockSpec((MOE_ROWS, f), lambda j, i, be, bf: (i, 0)),
                  pl.BlockSpec((None, f, tn), lambda j, i, be, bf: (be[i], 0, j)),
                  pl.BlockSpec((None, 1, tn), lambda j, i, be, bf: (be[i], 0, j))],
        out_specs=pl.BlockSpec((MOE_ROWS, tn), lambda j, i, be, bf: (i, j)),
        scratch_shapes=[pltpu.VMEM((f, tn), MXU_DTYPE)])
    return pl.pallas_call(
        _gmm2_kernel, grid_spec=gs,
        out_shape=jax.ShapeDtypeStruct((s, d), F32),
        compiler_params=_params(("arbitrary", "arbitrary")),
        name="moe_gmm2",
    )(blk_e, blk_flags, hs, w2, b2.reshape(N_EXPERTS, 1, d))


def _combine_kernel(dcur_ref, dnxt_ref, gate_ref, h_ref, g_ref, b_ref, ys_ref, o_ref, ob_ref, buf_ref, sem):
    i = pl.program_id(0)
    n_steps = pl.num_programs(0)
    n = h_ref.shape[0]
    slot = lax.rem(i, 2)

    def issue(d_ref, s):
        def body(r, carry):
            for k in range(TOP_K):
                d = d_ref[0, r * TOP_K + k]
                _row_copy(ys_ref.at[pl.ds(d, 1), :], buf_ref.at[s, k, pl.ds(r, 1), :], sem.at[s]).start()
            return carry
        lax.fori_loop(0, n, body, 0)

    @pl.when(i == 0)
    def _():
        issue(dcur_ref, 0)

    @pl.when(i + 1 < n_steps)
    def _():
        issue(dnxt_ref, 1 - slot)

    def drain(r, carry):
        for k in range(TOP_K):
            _row_copy(ys_ref.at[pl.ds(0, 1), :], buf_ref.at[slot, k, pl.ds(0, 1), :], sem.at[slot]).wait()
        return carry

    lax.fori_loop(0, n, drain, 0)
    gate = gate_ref[...]
    moe = gate[:, 0:1] * buf_ref[slot, 0]
    for k in range(1, TOP_K):
        moe = moe + gate[:, k:k + 1] * buf_ref[slot, k]
    y = _layer_norm(ALPHA * h_ref[...] + moe, g_ref[...], b_ref[...])
    o_ref[...] = y
    ob_ref[...] = y.astype(ob_ref.dtype)


def combine_ln(dest3, gate, h, ys, g, b):
    n, d = h.shape
    nt = n // MOE_TOK
    smem = lambda f: pl.BlockSpec((None, 1, MOE_TOK * TOP_K), f, memory_space=pltpu.SMEM)
    return pl.pallas_call(
        _combine_kernel,
        grid=(nt,),
        in_specs=[smem(lambda i: (i, 0, 0)),
                  smem(lambda i: (jnp.minimum(i + 1, nt - 1), 0, 0)),
                  pl.BlockSpec((MOE_TOK, 128), lambda i: (i, 0)),
                  pl.BlockSpec((MOE_TOK, d), lambda i: (i, 0)),
                  pl.BlockSpec((1, d), lambda i: (0, 0)),
                  pl.BlockSpec((1, d), lambda i: (0, 0)),
                  pl.BlockSpec(memory_space=pl.ANY)],
        out_specs=[pl.BlockSpec((MOE_TOK, d), lambda i: (i, 0)),
                   pl.BlockSpec((MOE_TOK, d), lambda i: (i, 0))],
        out_shape=[jax.ShapeDtypeStruct((n, d), F32), jax.ShapeDtypeStruct((n, d), MXU_DTYPE)],
        scratch_shapes=[pltpu.VMEM((2, TOP_K, MOE_TOK, d), F32), pltpu.SemaphoreType.DMA((2,))],
        compiler_params=_params(("arbitrary",)),
        name="moe_combine",
    )(dest3, dest3, gate, h, g.reshape(1, d), b.reshape(1, d), ys)


def moe_ln(h, hb, lw):
    n, d = h.shape
    idx, gate, cnt = router(hb, lw["w_router"], lw["b_router"])
    counts = cnt[0, :N_EXPERTS].astype(I32)
    padded = (counts + MOE_ROWS - 1) // MOE_ROWS * MOE_ROWS
    pad_end = jnp.cumsum(padded)
    pad_start = pad_end - padded
    dest = pad_start[idx[:, :TOP_K]] + idx[:, TOP_K:2 * TOP_K]
    dest3 = dest.reshape(n // MOE_TOK, 1, MOE_TOK * TOP_K)
    nblk = -(-(n * TOP_K) // MOE_ROWS) + N_EXPERTS
    blk_start = jnp.arange(nblk, dtype=I32) * MOE_ROWS
    blk_e = jnp.minimum(jnp.searchsorted(pad_end, blk_start, side="right"), N_EXPERTS - 1).astype(I32)
    valid = (blk_start < pad_end[-1]).astype(I32)
    first = jnp.concatenate([jnp.ones((1,), I32), (blk_e[1:] != blk_e[:-1]).astype(I32)])
    blk_flags = valid + 2 * first
    xs = dispatch(dest3, h, nblk * MOE_ROWS)
    hs = gmm1(blk_e, blk_flags, xs, lw["w1"], lw["b1"])
    ys = gmm2(blk_e, blk_flags, hs, lw["w2"], lw["b2"])
    return combine_ln(dest3, gate, h, ys, *lw["ln3"])


def _rows8(x):
    r = x.shape[0]
    return x if r == 8 else jnp.concatenate([x, jnp.zeros((8 - r, x.shape[1]), x.dtype)], axis=0)


def _paged_attend(lgs, lg_new, vs, v_new, exact=False):
    r = lg_new.shape[0]
    rnd = (lambda x: x) if exact else _rounded
    pv = _dot_exact if exact else _dot
    m = lg_new
    for lg in lgs:
        m = jnp.maximum(m, jnp.max(lg, axis=-1, keepdims=True))
    p_new = jnp.exp(lg_new - m)
    ps = [jnp.exp(lg - m) for lg in lgs]
    tot = p_new
    for p in ps:
        tot = tot + jnp.sum(p, axis=-1, keepdims=True)
    acc = rnd(p_new / tot) * rnd(v_new)
    for p, v in zip(ps, vs):
        acc = acc + pv(_rows8(p / tot), v)[0:r, :]
    return acc


def _page_specs(shape_tail, layer, n_pages):
    nd = len(shape_tail)
    return [pl.BlockSpec((None, None) + shape_tail,
                         functools.partial(lambda b, pt, pg: (layer, pt[b, pg]) + (0,) * nd, pg=pg))
            for pg in range(n_pages)]


def _dsa_dec_score_kernel(pt_ref, *refs, n_pages):
    del pt_ref
    pages = refs[:n_pages]
    iq_ref, iw_ref, new_ref, o_ref = refs[n_pages:]
    iq = iq_ref[...]
    iw = _rounded(iw_ref[...])
    for pg in range(n_pages):
        s = _dot_nt(iq, pages[pg][:, 2 * HD:2 * HD + IDX_DIM])
        o_ref[pg:pg + 1, :] = jnp.sum(_rounded(jnp.maximum(s, 0.0)) * iw, axis=0, keepdims=True)
    s_new = jnp.sum(iq.astype(F32) * _rounded(new_ref[:, 2 * HD:2 * HD + IDX_DIM]), axis=-1, keepdims=True)
    sc_new = jnp.sum(_rounded(jnp.maximum(s_new, 0.0)) * iw, axis=0, keepdims=True)
    o_ref[n_pages:n_pages + 1, :] = jnp.broadcast_to(sc_new, (1, PAGE))


def dsa_dec_scores(page_table, cache_dsa, layer, iq, iw, new_rows):
    nb, n_pages = page_table.shape
    gs = pltpu.PrefetchScalarGridSpec(
        num_scalar_prefetch=1, grid=(nb,),
        in_specs=_page_specs((PAGE, DSA_ROW), layer, n_pages) + [
            pl.BlockSpec((None, IDX_H, IDX_DIM), lambda b, pt: (b, 0, 0)),
            pl.BlockSpec((None, IDX_H, 1), lambda b, pt: (b, 0, 0)),
            pl.BlockSpec((None, 1, DSA_ROW), lambda b, pt: (b, 0, 0))],
        out_specs=pl.BlockSpec((None, n_pages + 1, PAGE), lambda b, pt: (b, 0, 0)))
    return pl.pallas_call(
        functools.partial(_dsa_dec_score_kernel, n_pages=n_pages), grid_spec=gs,
        out_shape=jax.ShapeDtypeStruct((nb, n_pages + 1, PAGE), F32),
        compiler_params=_params(("parallel",)),
        name="dsa_dec_scores",
    )(page_table, *([cache_dsa] * n_pages), iq, iw, new_rows)


def _topk_mask_kernel(s_ref, o_ref, *, n_keys, n_top):
    s = s_ref[...]
    valid = lax.broadcasted_iota(I32, s.shape, 1) < n_keys
    key = jnp.where(valid, _sort_key(s), INT_MIN)
    thr = _kth_largest_key(lambda c: jnp.sum(jnp.where(key >= c, 1.0, 0.0), axis=-1, keepdims=True),
                           n_top, (s.shape[0], 1))
    o_ref[...] = jnp.where(valid, jnp.where(key >= thr, 1.0, 0.0), 0.0)


def topk_mask(scores, n_keys, n_top):
    return pl.pallas_call(
        functools.partial(_topk_mask_kernel, n_keys=n_keys, n_top=n_top),
        out_shape=jax.ShapeDtypeStruct(scores.shape, F32),
        compiler_params=pltpu.CompilerParams(vmem_limit_bytes=V7X_VMEM_LIMIT),
        name="topk_mask",
    )(scores)


def _dsa_dec_attend_kernel(pt_ref, *refs, n_pages):
    del pt_ref
    pages = refs[:n_pages]
    q_ref, mask_ref, new_ref, o_ref = refs[n_pages:]
    q = q_ref[...]
    lgs, vs = [], []
    for pg in range(n_pages):
        lg = _dot_nt(_rows8(q), pages[pg][:, 0:HD])[0:DSA_H, :] * HD ** -0.5
        lgs.append(jnp.where(mask_ref[pg:pg + 1, :] > 0.5, lg, NEG_INF))
        vs.append(pages[pg][:, HD:2 * HD])
    lg_new = jnp.sum(q.astype(F32) * _rounded(new_ref[:, 0:HD]), axis=-1, keepdims=True) * HD ** -0.5
    lg_new = jnp.where(mask_ref[n_pages:n_pages + 1, 0:1] > 0.5, lg_new, NEG_INF)
    o_ref[...] = _paged_attend(lgs, lg_new, vs, new_ref[:, HD:2 * HD]).astype(o_ref.dtype)


def dsa_dec_attend(page_table, cache_dsa, layer, q, mask, new_rows):
    nb, n_pages = page_table.shape
    gs = pltpu.PrefetchScalarGridSpec(
        num_scalar_prefetch=1, grid=(nb,),
        in_specs=_page_specs((PAGE, DSA_ROW), layer, n_pages) + [
            pl.BlockSpec((None, DSA_H, HD), lambda b, pt: (b, 0, 0)),
            pl.BlockSpec((None, n_pages + 1, PAGE), lambda b, pt: (b, 0, 0)),
            pl.BlockSpec((None, 1, DSA_ROW), lambda b, pt: (b, 0, 0))],
        out_specs=pl.BlockSpec((None, DSA_H, HD), lambda b, pt: (b, 0, 0)))
    return pl.pallas_call(
        functools.partial(_dsa_dec_attend_kernel, n_pages=n_pages), grid_spec=gs,
        out_shape=jax.ShapeDtypeStruct((nb, DSA_H, HD), MXU_DTYPE),
        compiler_params=_params(("parallel",)),
        name="dsa_dec_attend",
    )(page_table, *([cache_dsa] * n_pages), q, mask, new_rows)


def _moba_dec_kernel(pt_ref, *refs, n_pages, n_sel):
    del pt_ref
    pages = refs[:n_pages]
    q_ref, new_ref, o_ref = refs[n_pages:]
    ppb = MOBA_BLOCK // PAGE
    nblk = n_pages // ppb
    for h in range(MOBA_H):
        q = q_ref[h:h + 1, :]
        ks = [pages[pg][:, h, 0:HD] for pg in range(n_pages)]
        lane = lax.broadcasted_iota(I32, (1, 128), 1)
        gsc = jnp.full((1, 128), NEG_INF, F32)
        for n in range(nblk):
            tot = jnp.sum(ks[n * ppb], axis=0, keepdims=True)
            for j in range(1, ppb):
                tot = tot + jnp.sum(ks[n * ppb + j], axis=0, keepdims=True)
            mean = tot * (1.0 / MOBA_BLOCK)
            gsc = jnp.where(lane == n, jnp.sum(q * mean, axis=-1, keepdims=True), gsc)
        selm = _top_rank_mask(gsc[:, 0:nblk], n_sel)
        lgs, vs = [], []
        for pg in range(n_pages):
            lg = _dot_nt_exact(_rows8(q), ks[pg])[0:1, :] * HD ** -0.5
            lgs.append(jnp.where(selm[:, pg // ppb:pg // ppb + 1] > 0.5, lg, NEG_INF))
            vs.append(pages[pg][:, h, HD:2 * HD])
        k_new = new_ref[:, h * 2 * HD:h * 2 * HD + HD]
        v_new = new_ref[:, h * 2 * HD + HD:(h + 1) * 2 * HD]
        lg_new = jnp.sum(q * k_new, axis=-1, keepdims=True) * HD ** -0.5
        o_ref[h:h + 1, :] = _paged_attend(lgs, lg_new, vs, v_new).astype(o_ref.dtype)


def moba_dec(page_table, cache_moba, layer, q, new_rows):
    nb, n_pages = page_table.shape
    n_keys = n_pages * PAGE + 1
    n_sel = min(MOBA_TOPK, (n_keys - 1) // MOBA_BLOCK)
    gs = pltpu.PrefetchScalarGridSpec(
        num_scalar_prefetch=1, grid=(nb,),
        in_specs=_page_specs((PAGE, MOBA_H, 2 * HD), layer, n_pages) + [
            pl.BlockSpec((None, MOBA_H, HD), lambda b, pt: (b, 0, 0)),
            pl.BlockSpec((None, 1, MOBA_H * 2 * HD), lambda b, pt: (b, 0, 0))],
        out_specs=pl.BlockSpec((None, MOBA_H, HD), lambda b, pt: (b, 0, 0)))
    return pl.pallas_call(
        functools.partial(_moba_dec_kernel, n_pages=n_pages, n_sel=n_sel), grid_spec=gs,
        out_shape=jax.ShapeDtypeStruct((nb, MOBA_H, HD), MXU_DTYPE),
        compiler_params=_params(("parallel",)),
        name="moba_dec",
    )(page_table, *([cache_moba] * n_pages), q, new_rows)


def _fox_dec_kernel(pt_ref, *refs, n_pages):
    del pt_ref
    pages = refs[:n_pages]
    q_ref, new_ref, post_ref, o_ref = refs[n_pages:]
    after = (lax.broadcasted_iota(I32, (PAGE, PAGE), 0) > lax.broadcasted_iota(I32, (PAGE, PAGE), 1))
    for h in range(FOX_H):
        q = q_ref[h:h + 1, :]
        later = post_ref[:, SM_FF + h:SM_FF + h + 1]
        lgs, vs = [None] * n_pages, [None] * n_pages
        for pg in range(n_pages - 1, -1, -1):
            lf = pages[pg][:, h, 2 * HD:2 * HD + 1]
            within = jnp.sum(jnp.where(after, lf, 0.0), axis=0, keepdims=True)
            lg = _dot_nt_exact(_rows8(q), pages[pg][:, h, 0:HD])[0:1, :] * HD ** -0.5
            lgs[pg] = lg + (within + later)
            vs[pg] = pages[pg][:, h, HD:2 * HD]
            later = later + jnp.sum(lf, axis=0, keepdims=True)
        k_new = new_ref[:, h * 2 * HD:h * 2 * HD + HD]
        v_new = new_ref[:, h * 2 * HD + HD:(h + 1) * 2 * HD]
        lg_new = jnp.sum(q * k_new, axis=-1, keepdims=True) * HD ** -0.5
        o_ref[h:h + 1, :] = _paged_attend(lgs, lg_new, vs, v_new, exact=True).astype(o_ref.dtype)


def fox_dec(page_table, cache_fox, layer, q, new_kv, post):
    nb, n_pages = page_table.shape
    gs = pltpu.PrefetchScalarGridSpec(
        num_scalar_prefetch=1, grid=(nb,),
        in_specs=_page_specs((PAGE, FOX_H, 2 * HD + 1), layer, n_pages) + [
            pl.BlockSpec((None, FOX_H, HD), lambda b, pt: (b, 0, 0)),
            pl.BlockSpec((None, 1, FOX_H * 2 * HD), lambda b, pt: (b, 0, 0)),
            pl.BlockSpec((None, 1, 128), lambda b, pt: (b, 0, 0))],
        out_specs=pl.BlockSpec((None, FOX_H, HD), lambda b, pt: (b, 0, 0)))
    return pl.pallas_call(
        functools.partial(_fox_dec_kernel, n_pages=n_pages), grid_spec=gs,
        out_shape=jax.ShapeDtypeStruct((nb, FOX_H, HD), MXU_DTYPE),
        compiler_params=_params(("parallel",)),
        name="fox_dec",
    )(page_table, *([cache_fox] * n_pages), q, new_kv, post)


def _ssd_dec_kernel(st_ref, cs_ref, xbc_ref, z_ref, post_ref, cw_ref, cb_ref, hp_ref, ng_ref,
                    o_ref, sto_ref, cso_ref):
    cs = cs_ref[...]
    xn = xbc_ref[...]
    cw = cw_ref[...]
    conv = cw[0:1, :] * cs[0:1, :]
    for j in range(1, SSM_CONV - 1):
        conv = conv + cw[j:j + 1, :] * cs[j:j + 1, :]
    conv = conv + cw[SSM_CONV - 1:SSM_CONV, :] * xn + cb_ref[...]
    cso_ref[0:SSM_CONV - 2, :] = cs[1:SSM_CONV - 1, :]
    cso_ref[SSM_CONV - 2:SSM_CONV - 1, :] = xn
    u = _silu(conv)
    post = post_ref[...]
    a_row = -jnp.exp(hp_ref[1:2, :])
    diag = lax.broadcasted_iota(I32, (SSM_P, SSM_P), 0) == lax.broadcasted_iota(I32, (SSM_P, SSM_P), 1)
    ys = []
    for h in range(SSM_H):
        g = h // (SSM_H // SSM_G)
        bg = u[:, SSM_DI + g * SSM_N:SSM_DI + (g + 1) * SSM_N]
        cg = u[:, SSM_DI + SSM_G * SSM_N + g * SSM_N:SSM_DI + SSM_G * SSM_N + (g + 1) * SSM_N]
        dt = post[:, SM_DT + h:SM_DT + h + 1]
        adt = dt * a_row[:, SM_DT + h:SM_DT + h + 1]
        x_h = u[:, h * SSM_P:(h + 1) * SSM_P]
        xdt = x_h * dt
        y_diag = xdt * jnp.sum(cg * bg, axis=-1, keepdims=True)
        xdt_col = jnp.sum(jnp.where(diag, xdt, 0.0), axis=1, keepdims=True)
        h0 = _rounded(st_ref[h])
        sto_ref[h] = _rounded(jnp.exp(adt)) * h0 + _rounded(xdt_col * bg)
        y_col = jnp.sum(h0 * _rounded(cg), axis=1, keepdims=True)
        y_off = jnp.exp(adt) * jnp.sum(jnp.where(diag, y_col, 0.0), axis=0, keepdims=True)
        ys.append(y_diag + y_off + x_h * hp_ref[2:3, SM_DT + h:SM_DT + h + 1])
    y = jnp.concatenate(ys, axis=1)
    o_ref[...] = _gated_group_norm(y, z_ref[...], ng_ref[...]).astype(o_ref.dtype)


def ssd_dec(state_ssm, state_conv, layer, xbc, z, post, conv_w, conv_b, head_params, norm_g):
    nb = xbc.shape[0]
    row = lambda w: pl.BlockSpec((None, 1, w), lambda b: (b, 0, 0))
    const = lambda r, w: pl.BlockSpec((r, w), lambda b: (0, 0))
    return pl.pallas_call(
        _ssd_dec_kernel,
        grid=(nb,),
        in_specs=[pl.BlockSpec((None, None, SSM_H, SSM_P, SSM_N), lambda b: (layer, b, 0, 0, 0)),
                  pl.BlockSpec((None, None, SSM_CONV - 1, CONV_DIM), lambda b: (layer, b, 0, 0)),
                  row(CONV_DIM), row(SSM_DI), row(128),
                  const(SSM_CONV, CONV_DIM), const(1, CONV_DIM), const(8, 128), const(1, SSM_DI)],
        out_specs=[row(SSM_DI),
                   pl.BlockSpec((None, SSM_H, SSM_P, SSM_N), lambda b: (b, 0, 0, 0)),
                   pl.BlockSpec((None, SSM_CONV - 1, CONV_DIM), lambda b: (b, 0, 0))],
        out_shape=[jax.ShapeDtypeStruct((nb, 1, SSM_DI), MXU_DTYPE),
                   jax.ShapeDtypeStruct((nb, SSM_H, SSM_P, SSM_N), F32),
                   jax.ShapeDtypeStruct((nb, SSM_CONV - 1, CONV_DIM), F32)],
        compiler_params=_params(("parallel",)),
        name="ssd_dec",
    )(state_ssm, state_conv, xbc.reshape(nb, 1, CONV_DIM), z.reshape(nb, 1, SSM_DI), post.reshape(nb, 1, 128),
      conv_w, conv_b.reshape(1, CONV_DIM), head_params, norm_g.reshape(1, SSM_DI))


def _mem_dec_kernel(q_ref, kv_ref, o_ref):
    for h in range(MEM_H):
        q = q_ref[h:h + 1, :]
        lg = _dot_nt(_rows8(q), kv_ref[:, h, 0:HD])[0:1, :] * HD ** -0.5
        m = jnp.max(lg, axis=-1, keepdims=True)
        p = jnp.exp(lg - m)
        o = _dot(_rows8(p / jnp.sum(p, axis=-1, keepdims=True)), kv_ref[:, h, HD:2 * HD])[0:1, :]
        o_ref[h:h + 1, :] = o.astype(o_ref.dtype)


def mem_dec(mq, cache_mem, layer):
    nb = mq.shape[0]
    mem_len = cache_mem.shape[2]
    return pl.pallas_call(
        _mem_dec_kernel,
        grid=(nb,),
        in_specs=[pl.BlockSpec((None, MEM_H, HD), lambda b: (b, 0, 0)),
                  pl.BlockSpec((None, None, mem_len, MEM_H, 2 * HD), lambda b: (layer, b, 0, 0, 0))],
        out_specs=pl.BlockSpec((None, MEM_H, HD), lambda b: (b, 0, 0)),
        out_shape=jax.ShapeDtypeStruct((nb, MEM_H, HD), MXU_DTYPE),
        compiler_params=_params(("parallel",)),
        name="mem_dec",
    )(mq, cache_mem)


def mixer_sample(xb, lw, layer, p):
    nb = xb.shape[0]
    page_table = p["page_table"]
    n_pages = page_table.shape[1]
    n_keys = n_pages * PAGE + 1
    pr = in_proj(xb, lw, F32)
    post, _ = small_post(pr["small"], lw["small_bias"], 1, nb)
    q_all = pr["q_all"]
    q_mxu = q_all[:, :Q_MOBA].astype(MXU_DTYPE)
    dsa_new = pr["dsa_row"].reshape(nb, 1, DSA_ROW)
    scores = dsa_dec_scores(page_table, p["cache_dsa"], layer,
                            q_mxu[:, Q_IDX:Q_IDX + IDX_H * IDX_DIM].reshape(nb, IDX_H, IDX_DIM),
                            post[:, SM_IW:SM_IW + IDX_H].reshape(nb, IDX_H, 1), dsa_new)
    mask = topk_mask(scores.reshape(nb, (n_pages + 1) * PAGE), n_keys, min(DSA_TOPK, n_keys // 4))
    o_dsa = dsa_dec_attend(page_table, p["cache_dsa"], layer, q_mxu[:, Q_DSA:Q_DSA + DSA_H * HD].reshape(nb, DSA_H, HD),
                           mask.reshape(nb, n_pages + 1, PAGE), dsa_new)
    o_moba = moba_dec(page_table, p["cache_moba"], layer, q_all[:, Q_MOBA:Q_MOBA + MOBA_H * HD].reshape(nb, MOBA_H, HD),
                      pr["moba_row"].reshape(nb, 1, MOBA_H * 2 * HD))
    o_fox = fox_dec(page_table, p["cache_fox"], layer, q_all[:, Q_FOX:Q_FOX + FOX_H * HD].reshape(nb, FOX_H, HD),
                    pr["fox_kv"].reshape(nb, 1, FOX_H * 2 * HD), post.reshape(nb, 1, 128))
    o_ssm, ssm_new, conv_new = ssd_dec(p["state_ssm"], p["state_conv"], layer, pr["xbc"], pr["z"], post,
                                       lw["conv_w"], lw["conv_b"], lw["head_params"], lw["norm_g"])
    logf = post[:, SM_FF:SM_FF + FOX_H].reshape(nb, 1, FOX_H, 1)
    rows = (dsa_new,
            pr["moba_row"].reshape(nb, 1, MOBA_H, 2 * HD),
            jnp.concatenate([pr["fox_kv"].reshape(nb, 1, FOX_H, 2 * HD), logf], axis=-1),
            ssm_new, conv_new)
    outs = (o_dsa.reshape(nb, DSA_H * HD), o_moba.reshape(nb, MOBA_H * HD), o_ssm.reshape(nb, SSM_DI),
            o_fox.reshape(nb, FOX_H * HD))
    return outs, rows


def kernel(x_prompt, x_sample, mem_prompt, cache_dsa, cache_moba, cache_fox, cache_mem, state_ssm, state_conv,
           page_table, w_in, fox_b_f, conv_w, conv_b, dt_bias, a_log, d_skip, ssm_norm_g, w_br_dsa, w_br_moba,
           w_br_ssm, w_br_fox, w_gate, b_gate, w_o, ln1_g, ln1_b, w_mq, w_mkv, w_mo, ln2_g, ln2_b, w_router,
           b_router, w1, b1, w2, b2, ln3_g, ln3_b):
    p = dict(cache_dsa=cache_dsa, cache_moba=cache_moba, cache_fox=cache_fox, cache_mem=cache_mem,
             state_ssm=state_ssm, state_conv=state_conv, page_table=page_table, w_in=w_in, fox_b_f=fox_b_f,
             conv_w=conv_w, conv_b=conv_b, dt_bias=dt_bias, a_log=a_log, d_skip=d_skip, ssm_norm_g=ssm_norm_g,
             w_br_dsa=w_br_dsa, w_br_moba=w_br_moba, w_br_ssm=w_br_ssm, w_br_fox=w_br_fox, w_gate=w_gate,
             b_gate=b_gate, w_o=w_o, ln1_g=ln1_g, ln1_b=ln1_b, w_mq=w_mq, w_mkv=w_mkv, w_mo=w_mo, ln2_g=ln2_g,
             ln2_b=ln2_b, w_router=w_router, b_router=b_router, w1=w1, b1=b1, w2=w2, b2=b2, ln3_g=ln3_g,
             ln3_b=ln3_b)
    nb, t, d = x_prompt.shape
    ns = x_sample.shape[0]
    mem_len = mem_prompt.shape[1]
    n_p = nb * t
    xp, xs = x_prompt.reshape(n_p, d), x_sample.reshape(ns, d)
    xpb, xsb = xp.astype(MXU_DTYPE), xs.astype(MXU_DTYPE)
    memb = mem_prompt.reshape(nb * mem_len, d).astype(MXU_DTYPE)
    st_p, st_s = [], []
    for l in range(DEPTH):
        lw = prep_layer(l, p)
        outs_p, rows_p = mixer_prompt(xpb, lw, nb, t)
        outs_s, rows_s = mixer_sample(xsb, lw, l, p)
        h1p, h1pb = mm_ln(gate_merge(xpb, outs_p, lw["w_gate"], lw["b_gate"], lw["projs"]), lw["w_o"], xp, *lw["ln1"])
        h1s, h1sb = mm_ln(gate_merge(xsb, outs_s, lw["w_gate"], lw["b_gate"], lw["projs"]), lw["w_o"], xs, *lw["ln1"])
        mem_kv = mm(memb, lw["w_mkv"])
        om_p = mem_prompt_attn(mm(h1pb, lw["w_mq"], MXU_DTYPE), mem_kv, nb, t, mem_len)
        om_s = mem_dec(mm(h1sb, lw["w_mq"], MXU_DTYPE).reshape(ns, MEM_H, HD), cache_mem, l).reshape(ns, MEM_H * HD)
        h2p, h2pb = mm_ln(om_p, lw["w_mo"], h1p, *lw["ln2"])
        h2s, h2sb = mm_ln(om_s, lw["w_mo"], h1s, *lw["ln2"])
        y, yb = moe_ln(jnp.concatenate([h2p, h2s], axis=0), jnp.concatenate([h2pb, h2sb], axis=0), lw)
        xp, xs, xpb, xsb = y[:n_p], y[n_p:], yb[:n_p], yb[n_p:]
        st_p.append(rows_p + (mem_kv.reshape(nb, mem_len, MEM_H, 2 * HD),))
        st_s.append(rows_s)
    sp = [jnp.stack(z) for z in zip(*st_p)]
    ss = [jnp.stack(z) for z in zip(*st_s)]
    return (xp.reshape(nb, t, d), xs.reshape(ns, 1, d), sp[0], ss[0], sp[1], ss[1], sp[2], ss[2], sp[3], ss[3],
            sp[4], ss[4], sp[5])
```

```python
import functools

import jax
import jax.numpy as jnp
from jax import lax
from jax.experimental import pallas as pl
from jax.experimental.pallas import tpu as pltpu

F32 = jnp.float32
I32 = jnp.int32
MXU_DTYPE = jnp.bfloat16
V7X_VMEM_LIMIT = 56 * 1024 * 1024

D_MODEL = 2048
DEPTH = 2
PAGE = 128
HD = 128
DSA_H, IDX_H, IDX_DIM, DSA_TOPK = 4, 16, 64, 256
DSA_ROW = 2 * HD + IDX_DIM
MOBA_H, MOBA_BLOCK, MOBA_TOPK = 4, 256, 3
SSM_H, SSM_P, SSM_G, SSM_N, SSM_CONV = 16, 64, 2, 128, 4
SSM_DI = SSM_H * SSM_P
CONV_DIM = SSM_DI + 2 * SSM_G * SSM_N
SSD_CHUNK = 256
FOX_H = 4
MEM_H = 4
N_EXPERTS, TOP_K, D_FF = 32, 4, 2048
SWIGLU_LIMIT, SWIGLU_ALPHA = 7.0, 1.702
ALPHA = (2 * DEPTH) ** 0.25
LN_EPS = 1e-5
RMS_EPS = 1e-5
IN_WIDTHS = (DSA_H * HD, 2 * HD, IDX_H * IDX_DIM, IDX_DIM, IDX_H, 3 * MOBA_H * HD,
             SSM_DI, CONV_DIM, SSM_H, 3 * FOX_H * HD, FOX_H)
SM_IW, SM_DT, SM_FF = 0, IDX_H, IDX_H + SSM_H
Q_IDX, Q_DSA, Q_MOBA, Q_FOX = 0, 1024, 1536, 2048
Q_WIDTH = 2560
MOE_ROWS = 512
NEG_INF = float("-inf")
INT_MIN = -2 ** 31


def _params(sem):
    return pltpu.CompilerParams(dimension_semantics=sem, vmem_limit_bytes=V7X_VMEM_LIMIT)


def _dot(a, b):
    return jnp.dot(a.astype(MXU_DTYPE), b.astype(MXU_DTYPE), preferred_element_type=F32)


def _dot_nt(a, b):
    return lax.dot_general(a.astype(MXU_DTYPE), b.astype(MXU_DTYPE), (((1,), (1,)), ((), ())),
                           preferred_element_type=F32)


def _dot_exact(a, b):
    return jnp.dot(a, b, preferred_element_type=F32, precision=lax.Precision.HIGHEST)


def _dot_nt_exact(a, b):
    return lax.dot_general(a, b, (((1,), (1,)), ((), ())), preferred_element_type=F32,
                           precision=lax.Precision.HIGHEST)


def _rounded(x):
    return x.astype(MXU_DTYPE).astype(F32)


def _sigmoid(x):
    return 1.0 / (1.0 + jnp.exp(-x))


def _softplus(x):
    return jnp.maximum(x, 0.0) + jnp.log1p(jnp.exp(-jnp.abs(x)))


def _log_sigmoid(x):
    return jnp.minimum(x, 0.0) - jnp.log1p(jnp.exp(-jnp.abs(x)))


def _silu(x):
    return x * _sigmoid(x)


def _pick(n, cands):
    for c in cands:
        if n % c == 0:
            return c
    return n


def _mm_kernel(x_ref, w_ref, o_ref):
    o_ref[...] = _dot(x_ref[...], w_ref[...]).astype(o_ref.dtype)


def mm(x, w, out_dtype=F32):
    m, k = x.shape
    n = w.shape[1]
    tm = _pick(m, (1024, 512, 256, 128))
    tn = _pick(n, (512, 384, 256, 128)) if n % 128 == 0 else n
    return pl.pallas_call(
        _mm_kernel,
        grid=(m // tm, n // tn),
        in_specs=[pl.BlockSpec((tm, k), lambda i, j: (i, 0)),
                  pl.BlockSpec((k, tn), lambda i, j: (0, j))],
        out_specs=pl.BlockSpec((tm, tn), lambda i, j: (i, j)),
        out_shape=jax.ShapeDtypeStruct((m, n), out_dtype),
        compiler_params=_params(("parallel", "arbitrary")),
        name="mm",
    )(x, w)


def _layer_norm(z, g, b):
    mu = jnp.mean(z, axis=-1, keepdims=True)
    zc = z - mu
    var = jnp.mean(zc * zc, axis=-1, keepdims=True)
    return zc * lax.rsqrt(var + LN_EPS) * g + b


def _mm_ln_kernel(a_ref, w_ref, r_ref, g_ref, b_ref, o_ref, ob_ref):
    z = ALPHA * r_ref[...] + _dot(a_ref[...], w_ref[...])
    y = _layer_norm(z, g_ref[...], b_ref[...])
    o_ref[...] = y
    ob_ref[...] = y.astype(ob_ref.dtype)


def mm_ln(a, w, resid, g, b):
    m, k = a.shape
    d = w.shape[1]
    tm = _pick(m, (512, 256, 128))
    return pl.pallas_call(
        _mm_ln_kernel,
        grid=(m // tm,),
        in_specs=[pl.BlockSpec((tm, k), lambda i: (i, 0)),
                  pl.BlockSpec((k, d), lambda i: (0, 0)),
                  pl.BlockSpec((tm, d), lambda i: (i, 0)),
                  pl.BlockSpec((1, d), lambda i: (0, 0)),
                  pl.BlockSpec((1, d), lambda i: (0, 0))],
        out_specs=[pl.BlockSpec((tm, d), lambda i: (i, 0)),
                   pl.BlockSpec((tm, d), lambda i: (i, 0))],
        out_shape=[jax.ShapeDtypeStruct((m, d), F32), jax.ShapeDtypeStruct((m, d), MXU_DTYPE)],
        compiler_params=_params(("parallel",)),
        name="mm_ln",
    )(a, w, resid, g.reshape(1, d), b.reshape(1, d))


def _gate_merge_kernel(x_ref, o0_ref, o1_ref, o2_ref, o3_ref, wg0_ref, wg1_ref, wg2_ref, wg3_ref,
                       bg0_ref, bg1_ref, bg2_ref, bg3_ref, p0_ref, p1_ref, p2_ref, p3_ref, m_ref):
    x = x_ref[...]
    acc = None
    for o_ref, wg_ref, bg_ref, p_ref in ((o0_ref, wg0_ref, bg0_ref, p0_ref), (o1_ref, wg1_ref, bg1_ref, p1_ref),
                                         (o2_ref, wg2_ref, bg2_ref, p2_ref), (o3_ref, wg3_ref, bg3_ref, p3_ref)):
        g = _sigmoid(_dot(x, wg_ref[...]) + bg_ref[...])
        t = g * _dot(o_ref[...], p_ref[...])
        acc = t if acc is None else acc + t
    m_ref[...] = acc.astype(m_ref.dtype)


def gate_merge(xb, outs, w_gate, b_gate, projs):
    m, d = xb.shape
    tm = _pick(m, (1024, 512, 256, 128))
    tn = 256
    nj = d // tn
    in_specs = [pl.BlockSpec((tm, d), lambda i, j: (i, 0))]
    in_specs += [pl.BlockSpec((tm, o.shape[1]), lambda i, j: (i, 0)) for o in outs]
    in_specs += [pl.BlockSpec((d, tn), functools.partial(lambda i, j, br: (0, br * nj + j), br=br)) for br in range(4)]
    in_specs += [pl.BlockSpec((1, tn), functools.partial(lambda i, j, br: (0, br * nj + j), br=br)) for br in range(4)]
    in_specs += [pl.BlockSpec((p.shape[0], tn), lambda i, j: (0, j)) for p in projs]
    bg = b_gate.reshape(1, 4 * d)
    return pl.pallas_call(
        _gate_merge_kernel,
        grid=(m // tm, nj),
        in_specs=in_specs,
        out_specs=pl.BlockSpec((tm, tn), lambda i, j: (i, j)),
        out_shape=jax.ShapeDtypeStruct((m, d), MXU_DTYPE),
        compiler_params=_params(("parallel", "arbitrary")),
        name="gate_merge",
    )(xb, *outs, w_gate, w_gate, w_gate, w_gate, bg, bg, bg, bg, *projs)


def _tri_incl(n):
    row = lax.broadcasted_iota(I32, (n, n), 0)
    col = lax.broadcasted_iota(I32, (n, n), 1)
    return col <= row


def _small_post_kernel(s_ref, bias_ref, post_ref, fcum_ref, carry_ref):
    c = pl.program_id(1)

    @pl.when(c == 0)
    def _():
        carry_ref[...] = jnp.zeros_like(carry_ref)

    raw = s_ref[...]
    z = raw + bias_ref[...]
    lane = lax.broadcasted_iota(I32, z.shape, 1)
    is_f = (lane >= SM_FF) & (lane < SM_FF + FOX_H)
    lf = jnp.where(is_f, _log_sigmoid(z), 0.0)
    post_ref[...] = jnp.where(lane < SM_DT, raw, jnp.where(lane < SM_FF, _softplus(z), lf))
    ck = z.shape[0]
    cs = _dot_exact(jnp.where(_tri_incl(ck), 1.0, 0.0), lf) + carry_ref[...]
    fcum_ref[...] = cs
    carry_ref[...] = cs[ck - 1:ck, :]


def small_post(small, bias_row, nb, t):
    ck = min(t, 256)
    nc = t // ck
    return pl.pallas_call(
        _small_post_kernel,
        grid=(nb, nc),
        in_specs=[pl.BlockSpec((ck, 128), lambda b, c: (b * nc + c, 0)),
                  pl.BlockSpec((1, 128), lambda b, c: (0, 0))],
        out_specs=[pl.BlockSpec((ck, 128), lambda b, c: (b * nc + c, 0)),
                   pl.BlockSpec((ck, 128), lambda b, c: (b * nc + c, 0))],
        out_shape=[jax.ShapeDtypeStruct(small.shape, F32), jax.ShapeDtypeStruct(small.shape, F32)],
        scratch_shapes=[pltpu.VMEM((1, 128), F32)],
        compiler_params=_params(("parallel", "arbitrary")),
        name="small_post",
    )(small, bias_row)


def _softmax_av(lg, v):
    m = jnp.max(lg, axis=-1, keepdims=True)
    p = jnp.exp(lg - m)
    return _dot(p / jnp.sum(p, axis=-1, keepdims=True), v)


def _sort_key(s):
    bits = lax.bitcast_convert_type(s + 0.0, I32)
    return bits ^ (lax.shift_right_arithmetic(bits, 31) & 0x7FFFFFFF)


def _kth_largest_key(count_ge, k, shape):
    kf = float(k)
    t0 = jnp.where(count_ge(jnp.zeros(shape, I32)) >= kf, 0, INT_MIN).astype(I32)

    def body(i, t):
        cand = t | lax.shift_left(jnp.int32(1), 30 - i)
        return jnp.where(count_ge(cand) >= kf, cand, t)

    return lax.fori_loop(0, 31, body, t0)


def _top_rank_mask(gsc, n_sel):
    nb = gsc.shape[1]
    blk = lax.broadcasted_iota(I32, gsc.shape, 1)
    rank = jnp.zeros(gsc.shape, F32)
    for m in range(nb):
        cm = gsc[:, m:m + 1]
        beats = jnp.where(cm > gsc, 1.0, jnp.where(cm == gsc, jnp.where(blk > m, 1.0, 0.0), 0.0))
        rank = rank + beats
    return jnp.where(rank < float(n_sel), jnp.where(gsc > NEG_INF, 1.0, 0.0), 0.0)


def _dsa_prompt_kernel(iq_ref, q_ref, sm_ref, row_ref, o_ref, *, n_top):
    qi = pl.program_id(1)
    tq = q_ref.shape[0]
    t = row_ref.shape[0]
    kidx = row_ref[:, 2 * HD:2 * HD + IDX_DIM].astype(MXU_DTYPE)
    iq = iq_ref[...]
    iw = _rounded(sm_ref[:, SM_IW:SM_IW + IDX_H])
    score = jnp.zeros((tq, t), F32)
    for h in range(IDX_H):
        s = _dot_nt(iq[:, h * IDX_DIM:(h + 1) * IDX_DIM], kidx)
        score = score + _rounded(jnp.maximum(s, 0.0)) * iw[:, h:h + 1]
    tpos = qi * tq + lax.broadcasted_iota(I32, (tq, t), 0)
    lpos = lax.broadcasted_iota(I32, (tq, t), 1)
    causal = lpos <= tpos
    key = jnp.where(causal, _sort_key(score), INT_MIN)
    thr = _kth_largest_key(lambda c: jnp.sum(jnp.where(key >= c, 1.0, 0.0), axis=-1, keepdims=True),
                           n_top, (tq, 1))
    sel = key >= thr
    kb = row_ref[:, 0:HD].astype(MXU_DTYPE)
    vb = row_ref[:, HD:2 * HD].astype(MXU_DTYPE)
    for h in range(DSA_H):
        lg = _dot_nt(q_ref[:, h * HD:(h + 1) * HD], kb) * HD ** -0.5
        lg = jnp.where(causal, jnp.where(sel, lg, NEG_INF), NEG_INF)
        o_ref[:, h * HD:(h + 1) * HD] = _softmax_av(lg, vb).astype(o_ref.dtype)


def dsa_prompt(q_all, post, dsa_row, nb, t, n_top):
    tq = min(t, 256)
    nq = t // tq
    return pl.pallas_call(
        functools.partial(_dsa_prompt_kernel, n_top=n_top),
        grid=(nb, nq),
        in_specs=[pl.BlockSpec((tq, IDX_H * IDX_DIM), lambda b, i: (b * nq + i, Q_IDX // 1024)),
                  pl.BlockSpec((tq, DSA_H * HD), lambda b, i: (b * nq + i, Q_DSA // 512)),
                  pl.BlockSpec((tq, 128), lambda b, i: (b * nq + i, 0)),
                  pl.BlockSpec((t, DSA_ROW), lambda b, i: (b, 0))],
        out_specs=pl.BlockSpec((tq, DSA_H * HD), lambda b, i: (b * nq + i, 0)),
        out_shape=jax.ShapeDtypeStruct((nb * t, DSA_H * HD), MXU_DTYPE),
        compiler_params=_params(("parallel", "arbitrary")),
        name="dsa_prompt",
    )(q_all, q_all, post, dsa_row)


def _moba_prompt_kernel(q_ref, kv_ref, o_ref, kb_ref, vb_ref, mean_ref, *, n_sel):
    qi = pl.program_id(2)
    t = kv_ref.shape[0]
    nblk = t // MOBA_BLOCK

    @pl.when(qi == 0)
    def _():
        kb_ref[...] = kv_ref[:, 0:HD].astype(MXU_DTYPE)
        vb_ref[...] = kv_ref[:, HD:2 * HD].astype(MXU_DTYPE)
        for n in range(nblk):
            mean_ref[n:n + 1, :] = jnp.mean(kv_ref[n * MOBA_BLOCK:(n + 1) * MOBA_BLOCK, 0:HD], axis=0,
                                            keepdims=True)

    q = q_ref[...]
    logits = _dot_nt(q, kb_ref[...]) * HD ** -0.5
    tq = q.shape[0]
    own_causal = (lax.broadcasted_iota(I32, (tq, MOBA_BLOCK), 1) <= lax.broadcasted_iota(I32, (tq, MOBA_BLOCK), 0))
    if n_sel:
        gsc = _dot_nt(q, mean_ref[...])
        gsc = jnp.where(lax.broadcasted_iota(I32, gsc.shape, 1) < qi, gsc, NEG_INF)
        selm = _top_rank_mask(gsc, n_sel)
    pieces = []
    for n in range(nblk):
        lg = logits[:, n * MOBA_BLOCK:(n + 1) * MOBA_BLOCK]
        piece = jnp.where(qi == n, jnp.where(own_causal, lg, NEG_INF), NEG_INF)
        if n_sel:
            piece = jnp.where(selm[:, n:n + 1] > 0.5, lg, piece)
        pieces.append(piece)
    lg = jnp.concatenate(pieces, axis=1) if nblk > 1 else pieces[0]
    o_ref[...] = _softmax_av(lg, vb_ref[...]).astype(o_ref.dtype)


def moba_prompt(q_all, moba_row, nb, t, n_sel):
    nq = t // MOBA_BLOCK
    return pl.pallas_call(
        functools.partial(_moba_prompt_kernel, n_sel=n_sel),
        grid=(nb, MOBA_H, nq),
        in_specs=[pl.BlockSpec((MOBA_BLOCK, HD), lambda b, h, i: (b * nq + i, Q_MOBA // HD + h)),
                  pl.BlockSpec((t, 2 * HD), lambda b, h, i: (b, h))],
        out_specs=pl.BlockSpec((MOBA_BLOCK, HD), lambda b, h, i: (b * nq + i, h)),
        out_shape=jax.ShapeDtypeStruct((nb * t, MOBA_H * HD), MXU_DTYPE),
        scratch_shapes=[pltpu.VMEM((t, HD), MXU_DTYPE), pltpu.VMEM((t, HD), MXU_DTYPE),
                        pltpu.VMEM((nq, HD), F32)],
        compiler_params=_params(("parallel", "parallel", "arbitrary")),
        name="moba_prompt",
    )(q_all, moba_row)


def _fox_prompt_kernel(q_ref, kv_ref, fc_ref, fk_ref, o_ref, kb_ref, vb_ref):
    h = pl.program_id(1)
    qi = pl.program_id(2)

    @pl.when(qi == 0)
    def _():
        kb_ref[...] = kv_ref[:, 0:HD].astype(MXU_DTYPE)
        vb_ref[...] = kv_ref[:, HD:2 * HD].astype(MXU_DTYPE)

    tq = q_ref.shape[0]
    t = kv_ref.shape[0]
    fc = fc_ref[...]
    fq = jnp.sum(jnp.where(lax.broadcasted_iota(I32, fc.shape, 1) == SM_FF + h, fc, 0.0), axis=-1, keepdims=True)
    lg = _dot_nt(q_ref[...], kb_ref[...]) * HD ** -0.5
    lg = lg + fq - fk_ref[...]
    causal = lax.broadcasted_iota(I32, (tq, t), 1) <= qi * tq + lax.broadcasted_iota(I32, (tq, t), 0)
    lg = jnp.where(causal, lg, NEG_INF)
    o_ref[...] = _softmax_av(lg, vb_ref[...]).astype(o_ref.dtype)


def fox_prompt(q_all, fox_kv, fcum, fk_rows, nb, t):
    tq = min(t, 256)
    nq = t // tq
    return pl.pallas_call(
        _fox_prompt_kernel,
        grid=(nb, FOX_H, nq),
        in_specs=[pl.BlockSpec((tq, HD), lambda b, h, i: (b * nq + i, Q_FOX // HD + h)),
                  pl.BlockSpec((t, 2 * HD), lambda b, h, i: (b, h)),
                  pl.BlockSpec((tq, 128), lambda b, h, i: (b * nq + i, 0)),
                  pl.BlockSpec((None, 1, t), lambda b, h, i: (b * FOX_H + h, 0, 0))],
        out_specs=pl.BlockSpec((tq, HD), lambda b, h, i: (b * nq + i, h)),
        out_shape=jax.ShapeDtypeStruct((nb * t, FOX_H * HD), MXU_DTYPE),
        scratch_shapes=[pltpu.VMEM((t, HD), MXU_DTYPE), pltpu.VMEM((t, HD), MXU_DTYPE)],
        compiler_params=_params(("parallel", "parallel", "arbitrary")),
        name="fox_prompt",
    )(q_all, fox_kv, fcum, fk_rows)


def _gated_group_norm(y, z, ng):
    y = y * _silu(z)
    gw = SSM_DI // SSM_G
    parts = []
    for g in range(SSM_G):
        yg = y[:, g * gw:(g + 1) * gw]
        parts.append(yg * lax.rsqrt(jnp.mean(yg * yg, axis=-1, keepdims=True) + RMS_EPS))
    return jnp.concatenate(parts, axis=1) * ng


def _chunk_decays(alast_ref, z, nc):
    rows = []
    for j in range(nc):
        e = jnp.zeros((1, 128), F32)
        for m in range(j + 1, nc):
            e = e + jnp.where(m < z, alast_ref[m], 0.0)
        rows.append(jnp.where(j < z, _rounded(jnp.exp(e)), 0.0))
    return rows


def _carried_state(dec, sloc_ref, h):
    acc = None
    for j, d in enumerate(dec):
        t = d[:, SM_DT + h:SM_DT + h + 1] * sloc_ref[j, h].astype(F32)
        acc = t if acc is None else acc + t
    return acc


def _ssd_prompt_kernel(xbc_ref, z_ref, post_ref, cw_ref, cb_ref, hp_ref, ng_ref, o_ref, st_ref,
                       sloc_ref, alast_ref, tail_ref):
    c = pl.program_id(1)
    nc = sloc_ref.shape[0]
    ln = xbc_ref.shape[0]

    @pl.when(c == 0)
    def _():
        sloc_ref[...] = jnp.zeros_like(sloc_ref)
        alast_ref[...] = jnp.zeros_like(alast_ref)
        tail_ref[...] = jnp.zeros_like(tail_ref)

    xin = xbc_ref[...]
    xx = _rounded(jnp.concatenate([tail_ref[...], xin], axis=0))
    cw = _rounded(cw_ref[...])
    conv = cw[0:1, :] * xx[5:5 + ln, :]
    for j in range(1, SSM_CONV):
        conv = conv + cw[j:j + 1, :] * xx[5 + j:5 + j + ln, :]
    conv = conv + cb_ref[...]
    tail_ref[...] = xin[ln - 8:ln, :]
    u = _silu(conv)
    xs = u[:, :SSM_DI]
    xs_t = xs.T
    lane = lax.broadcasted_iota(I32, (ln, 128), 1)
    is_dt = (lane >= SM_DT) & (lane < SM_DT + SSM_H)
    dt = jnp.where(is_dt, post_ref[...], 0.0)
    a_row = -jnp.exp(hp_ref[1:2, :])
    adt = jnp.where(is_dt, dt * a_row, 0.0)
    low = _tri_incl(ln)
    acs = _dot_exact(jnp.where(low, 1.0, 0.0), adt)
    acs_t = acs.T
    dt_t = dt.T
    a_last = acs[ln - 1:ln, :]
    dec = _chunk_decays(alast_ref, c, nc)
    ys = []
    for g in range(SSM_G):
        bg = u[:, SSM_DI + g * SSM_N:SSM_DI + (g + 1) * SSM_N]
        cg = u[:, SSM_DI + SSM_G * SSM_N + g * SSM_N:SSM_DI + SSM_G * SSM_N + (g + 1) * SSM_N]
        cb = _dot_nt(cg, bg)
        for hh in range(SSM_H // SSM_G):
            h = g * (SSM_H // SSM_G) + hh
            col = acs[:, SM_DT + h:SM_DT + h + 1]
            row = acs_t[SM_DT + h:SM_DT + h + 1, :]
            last = a_last[:, SM_DT + h:SM_DT + h + 1]
            x_h = xs[:, h * SSM_P:(h + 1) * SSM_P]
            xdt = x_h * dt[:, SM_DT + h:SM_DT + h + 1]
            scores = cb * jnp.where(low, jnp.exp(col - row), 0.0)
            y = _dot(scores, xdt) + jnp.exp(col) * _dot_nt(cg, _carried_state(dec, sloc_ref, h))
            ys.append(y + x_h * hp_ref[2:3, SM_DT + h:SM_DT + h + 1])
            xdt_t = (xs_t[h * SSM_P:(h + 1) * SSM_P, :] * dt_t[SM_DT + h:SM_DT + h + 1, :]) * jnp.exp(last - row)
            sloc_ref[c, h] = _dot(xdt_t, bg).astype(sloc_ref.dtype)
    alast_ref[c] = a_last
    y = jnp.concatenate(ys, axis=1)
    o_ref[...] = _gated_group_norm(y, z_ref[...], ng_ref[...]).astype(o_ref.dtype)

    @pl.when(c == nc - 1)
    def _():
        dec_end = _chunk_decays(alast_ref, nc, nc)
        for h in range(SSM_H):
            st_ref[h] = _carried_state(dec_end, sloc_ref, h)


def ssd_prompt(xbc, z, post, conv_w, conv_b, head_params, norm_g, nb, t):
    ck = min(t, SSD_CHUNK)
    nc = t // ck
    return pl.pallas_call(
        _ssd_prompt_kernel,
        grid=(nb, nc),
        in_specs=[pl.BlockSpec((ck, CONV_DIM), lambda b, c: (b * nc + c, 0)),
                  pl.BlockSpec((ck, SSM_DI), lambda b, c: (b * nc + c, 0)),
                  pl.BlockSpec((ck, 128), lambda b, c: (b * nc + c, 0)),
                  pl.BlockSpec((SSM_CONV, CONV_DIM), lambda b, c: (0, 0)),
                  pl.BlockSpec((1, CONV_DIM), lambda b, c: (0, 0)),
                  pl.BlockSpec((8, 128), lambda b, c: (0, 0)),
                  pl.BlockSpec((1, SSM_DI), lambda b, c: (0, 0))],
        out_specs=[pl.BlockSpec((ck, SSM_DI), lambda b, c: (b * nc + c, 0)),
                   pl.BlockSpec((None, SSM_H, SSM_P, SSM_N), lambda b, c: (b, 0, 0, 0))],
        out_shape=[jax.ShapeDtypeStruct((nb * t, SSM_DI), MXU_DTYPE),
                   jax.ShapeDtypeStruct((nb, SSM_H, SSM_P, SSM_N), F32)],
        scratch_shapes=[pltpu.VMEM((nc, SSM_H, SSM_P, SSM_N), MXU_DTYPE), pltpu.VMEM((nc, 1, 128), F32),
                        pltpu.VMEM((8, CONV_DIM), F32)],
        compiler_params=_params(("parallel", "arbitrary")),
        name="ssd_prompt",
    )(xbc, z, post, conv_w, conv_b.reshape(1, CONV_DIM), head_params, norm_g.reshape(1, SSM_DI))


def _mem_prompt_kernel(q_ref, kv_ref, o_ref):
    for h in range(MEM_H):
        k = kv_ref[:, h * 2 * HD:h * 2 * HD + HD]
        v = kv_ref[:, h * 2 * HD + HD:(h + 1) * 2 * HD]
        lg = _dot_nt(q_ref[:, h * HD:(h + 1) * HD], k) * HD ** -0.5
        o_ref[:, h * HD:(h + 1) * HD] = _softmax_av(lg, v).astype(o_ref.dtype)


def mem_prompt_attn(mq, mem_kv, nb, t, mem_len):
    tq = _pick(t, (512, 256, 128))
    nq = t // tq
    return pl.pallas_call(
        _mem_prompt_kernel,
        grid=(nb, nq),
        in_specs=[pl.BlockSpec((tq, MEM_H * HD), lambda b, i: (b * nq + i, 0)),
                  pl.BlockSpec((mem_len, MEM_H * 2 * HD), lambda b, i: (b, 0))],
        out_specs=pl.BlockSpec((tq, MEM_H * HD), lambda b, i: (b * nq + i, 0)),
        out_shape=jax.ShapeDtypeStruct((nb * t, MEM_H * HD), MXU_DTYPE),
        compiler_params=_params(("parallel", "arbitrary")),
        name="mem_prompt",
    )(mq, mem_kv)


def _split_cols(w, widths):
    out, o = [], 0
    for wd in widths:
        out.append(w[:, o:o + wd])
        o += wd
    return out


def _interleave_heads(k, v, n_heads):
    pieces = []
    for h in range(n_heads):
        pieces += [k[:, h * HD:(h + 1) * HD], v[:, h * HD:(h + 1) * HD]]
    return jnp.concatenate(pieces, axis=1)


def prep_layer(l, p):
    c = lambda a: a.astype(MXU_DTYPE)
    (dsa_q, dsa_kv, idx_q, idx_k, idx_w, moba_qkv, ssm_z, ssm_xbc, ssm_dt, fox_qkv, fox_f) = _split_cols(
        p["w_in"][l], IN_WIDTHS)
    moba_q, moba_k, moba_v = _split_cols(moba_qkv, (MOBA_H * HD,) * 3)
    fox_q, fox_k, fox_v = _split_cols(fox_qkv, (FOX_H * HD,) * 3)
    pad = jnp.zeros((D_MODEL, 128 - SM_FF - FOX_H), F32)
    small_bias = jnp.zeros((1, 128), F32)
    small_bias = small_bias.at[0, SM_DT:SM_DT + SSM_H].set(p["dt_bias"][l])
    small_bias = small_bias.at[0, SM_FF:SM_FF + FOX_H].set(p["fox_b_f"][l])
    head_params = jnp.zeros((8, 128), F32)
    head_params = head_params.at[1, SM_DT:SM_DT + SSM_H].set(p["a_log"][l])
    head_params = head_params.at[2, SM_DT:SM_DT + SSM_H].set(p["d_skip"][l])
    router_pad = jnp.zeros((D_MODEL, 128 - N_EXPERTS), F32)
    return dict(
        w_q=c(jnp.concatenate([idx_q * IDX_DIM ** -0.5, dsa_q, moba_q, fox_q], axis=1)),
        w_dsa_row=c(jnp.concatenate([dsa_kv, idx_k], axis=1)),
        w_moba_row=c(_interleave_heads(moba_k, moba_v, MOBA_H)),
        w_fox_kv=c(_interleave_heads(fox_k, fox_v, FOX_H)),
        w_z=c(ssm_z), w_xbc=c(ssm_xbc),
        w_small=c(jnp.concatenate([idx_w * IDX_H ** -0.5, ssm_dt, fox_f, pad], axis=1)),
        small_bias=small_bias, head_params=head_params,
        conv_w=p["conv_w"][l], conv_b=p["conv_b"][l], norm_g=p["ssm_norm_g"][l],
        projs=[c(p["w_br_dsa"][l]), c(p["w_br_moba"][l]), c(p["w_br_ssm"][l]), c(p["w_br_fox"][l])],
        w_gate=c(p["w_gate"][l]), b_gate=p["b_gate"][l], w_o=c(p["w_o"][l]),
        ln1=(p["ln1_g"][l], p["ln1_b"][l]), ln2=(p["ln2_g"][l], p["ln2_b"][l]), ln3=(p["ln3_g"][l], p["ln3_b"][l]),
        w_mq=c(p["w_mq"][l]), w_mkv=c(p["w_mkv"][l]), w_mo=c(p["w_mo"][l]),
        w_router=c(jnp.concatenate([p["w_router"][l], router_pad], axis=1)),
        b_router=jnp.concatenate([p["b_router"][l], jnp.full((128 - N_EXPERTS,), NEG_INF, F32)]).reshape(1, 128),
        w1=p["w1"], b1=p["b1"], w2=p["w2"], b2=p["b2"], layer=l,
    )


def in_proj(xb, lw, q_dtype=MXU_DTYPE):
    return dict(
        q_all=mm(xb, lw["w_q"], q_dtype),
        dsa_row=mm(xb, lw["w_dsa_row"]),
        moba_row=mm(xb, lw["w_moba_row"]),
        fox_kv=mm(xb, lw["w_fox_kv"]),
        z=mm(xb, lw["w_z"]),
        xbc=mm(xb, lw["w_xbc"]),
        small=mm(xb, lw["w_small"]),
    )


def mixer_prompt(xb, lw, nb, t):
    pr = in_proj(xb, lw)
    post, fcum = small_post(pr["small"], lw["small_bias"], nb, t)
    o_dsa = dsa_prompt(pr["q_all"], post, pr["dsa_row"], nb, t, min(DSA_TOPK, t // 4))
    o_moba = moba_prompt(pr["q_all"], pr["moba_row"], nb, t, min(MOBA_TOPK, (t - 1) // MOBA_BLOCK))
    o_ssm, ssm_new = ssd_prompt(pr["xbc"], pr["z"], post, lw["conv_w"], lw["conv_b"], lw["head_params"],
                                lw["norm_g"], nb, t)
    fk_rows = fcum[:, SM_FF:SM_FF + FOX_H].reshape(nb, t, FOX_H).transpose(0, 2, 1).reshape(nb * FOX_H, 1, t)
    o_fox = fox_prompt(pr["q_all"], pr["fox_kv"], fcum, fk_rows, nb, t)
    logf = post[:, SM_FF:SM_FF + FOX_H].reshape(nb, t, FOX_H, 1)
    rows = (pr["dsa_row"].reshape(nb, t, DSA_ROW),
            pr["moba_row"].reshape(nb, t, MOBA_H, 2 * HD),
            jnp.concatenate([pr["fox_kv"].reshape(nb, t, FOX_H, 2 * HD), logf], axis=-1),
            ssm_new,
            pr["xbc"].reshape(nb, t, CONV_DIM)[:, t - (SSM_CONV - 1):])
    return (o_dsa, o_moba, o_ssm, o_fox), rows


MOE_TOK = 128


def _router_kernel(h_ref, w_ref, b_ref, idx_ref, gate_ref, cnt_ref, carry_ref):
    i = pl.program_id(0)

    @pl.when(i == 0)
    def _():
        carry_ref[...] = jnp.zeros_like(carry_ref)

    work = _dot(h_ref[...], w_ref[...]) + b_ref[...]
    tm = work.shape[0]
    lane = lax.broadcasted_iota(I32, work.shape, 1)
    hot = jnp.zeros(work.shape, F32)
    vals, idxs = [], []
    for _k in range(TOP_K):
        m = jnp.max(work, axis=-1, keepdims=True)
        idx = jnp.min(jnp.where(work == m, lane, 128), axis=-1, keepdims=True)
        vals.append(m)
        idxs.append(idx)
        hot = hot + jnp.where(lane == idx, 1.0, 0.0)
        work = jnp.where(lane == idx, NEG_INF, work)
    es = [jnp.exp(v - vals[0]) for v in vals]
    tot = es[0] + es[1] + es[2] + es[3]
    strict = lax.broadcasted_iota(I32, (tm, tm), 1) < lax.broadcasted_iota(I32, (tm, tm), 0)
    prefix = _dot(jnp.where(strict, 1.0, 0.0), hot) + carry_ref[...]
    out_i = jnp.zeros(work.shape, I32)
    out_g = jnp.zeros(work.shape, F32)
    for k in range(TOP_K):
        rank = jnp.sum(jnp.where(lane == idxs[k], prefix, 0.0), axis=-1, keepdims=True)
        out_i = jnp.where(lane == k, idxs[k], out_i)
        out_i = jnp.where(lane == TOP_K + k, rank.astype(I32), out_i)
        out_g = jnp.where(lane == k, es[k] / tot, out_g)
    idx_ref[...] = out_i
    gate_ref[...] = out_g
    carry_ref[...] = carry_ref[...] + jnp.sum(hot, axis=0, keepdims=True)
    cnt_ref[...] = carry_ref[...]


def router(hb, w_router, b_router):
    n, d = hb.shape
    tm = MOE_TOK
    return pl.pallas_call(
        _router_kernel,
        grid=(n // tm,),
        in_specs=[pl.BlockSpec((tm, d), lambda i: (i, 0)),
                  pl.BlockSpec((d, 128), lambda i: (0, 0)),
                  pl.BlockSpec((1, 128), lambda i: (0, 0))],
        out_specs=[pl.BlockSpec((tm, 128), lambda i: (i, 0)),
                   pl.BlockSpec((tm, 128), lambda i: (i, 0)),
                   pl.BlockSpec((1, 128), lambda i: (0, 0))],
        out_shape=[jax.ShapeDtypeStruct((n, 128), I32), jax.ShapeDtypeStruct((n, 128), F32),
                   jax.ShapeDtypeStruct((1, 128), F32)],
        scratch_shapes=[pltpu.VMEM((1, 128), F32)],
        compiler_params=_params(("arbitrary",)),
        name="router",
    )(hb, w_router, b_router)


def _row_copy(src, dst, sem):
    return pltpu.make_async_copy(src, dst, sem)


def _dispatch_kernel(dest_ref, x_ref, xs_in_ref, xs_ref, sem):
    del xs_in_ref
    n = x_ref.shape[0]

    def issue(r, carry):
        for k in range(TOP_K):
            d = dest_ref[0, r * TOP_K + k]
            _row_copy(x_ref.at[pl.ds(r, 1), :], xs_ref.at[pl.ds(d, 1), :], sem).start()
        return carry

    lax.fori_loop(0, n, issue, 0)

    def drain(r, carry):
        for _k in range(TOP_K):
            _row_copy(x_ref.at[pl.ds(0, 1), :], xs_ref.at[pl.ds(0, 1), :], sem).wait()
        return carry

    lax.fori_loop(0, n, drain, 0)


def dispatch(dest3, x, n_rows):
    n, d = x.shape
    xs0 = jnp.zeros((n_rows, d), x.dtype)
    return pl.pallas_call(
        _dispatch_kernel,
        grid=(n // MOE_TOK,),
        in_specs=[pl.BlockSpec((None, 1, MOE_TOK * TOP_K), lambda i: (i, 0, 0), memory_space=pltpu.SMEM),
                  pl.BlockSpec((MOE_TOK, d), lambda i: (i, 0)),
                  pl.BlockSpec(memory_space=pl.ANY)],
        out_specs=pl.BlockSpec(memory_space=pl.ANY),
        out_shape=jax.ShapeDtypeStruct((n_rows, d), x.dtype),
        scratch_shapes=[pltpu.SemaphoreType.DMA(())],
        input_output_aliases={2: 0},
        compiler_params=_params(("arbitrary",)),
        name="moe_dispatch",
    )(dest3, x, xs0)


def _gmm1_kernel(be_ref, bf_ref, xs_ref, wg_ref, wl_ref, bg_ref, bl_ref, o_ref, wgb_ref, wlb_ref):
    del be_ref
    i = pl.program_id(1)
    flags = bf_ref[i]

    @pl.when((flags & 2) != 0)
    def _():
        wgb_ref[...] = wg_ref[...].astype(wgb_ref.dtype)
        wlb_ref[...] = wl_ref[...].astype(wlb_ref.dtype)

    @pl.when((flags & 1) != 0)
    def _():
        x = xs_ref[...]
        g = jnp.minimum(_dot(x, wgb_ref[...]) + bg_ref[...], SWIGLU_LIMIT)
        lin = jnp.clip(_dot(x, wlb_ref[...]) + bl_ref[...], -SWIGLU_LIMIT, SWIGLU_LIMIT)
        o_ref[...] = (g * _sigmoid(SWIGLU_ALPHA * g) * (lin + 1.0)).astype(o_ref.dtype)

    @pl.when((flags & 1) == 0)
    def _():
        o_ref[...] = jnp.zeros_like(o_ref)


def gmm1(blk_e, blk_flags, xs, w1, b1, layer):
    s, d = xs.shape
    nblk = s // MOE_ROWS
    tn = 512
    nj = D_FF // tn
    gs = pltpu.PrefetchScalarGridSpec(
        num_scalar_prefetch=2, grid=(nj, nblk),
        in_specs=[pl.BlockSpec((MOE_ROWS, d), lambda j, i, be, bf: (i, 0)),
                  pl.BlockSpec((None, None, d, tn), lambda j, i, be, bf: (layer, be[i], 0, j)),
                  pl.BlockSpec((None, None, d, tn), lambda j, i, be, bf: (layer, be[i], 0, nj + j)),
                  pl.BlockSpec((None, None, 1, tn), lambda j, i, be, bf: (layer, be[i], 0, j)),
                  pl.BlockSpec((None, None, 1, tn), lambda j, i, be, bf: (layer, be[i], 0, nj + j))],
        out_specs=pl.BlockSpec((MOE_ROWS, tn), lambda j, i, be, bf: (i, j)),
        scratch_shapes=[pltpu.VMEM((d, tn), MXU_DTYPE), pltpu.VMEM((d, tn), MXU_DTYPE)])
    b1r = b1.reshape(DEPTH, N_EXPERTS, 1, 2 * D_FF)
    return pl.pallas_call(
        _gmm1_kernel, grid_spec=gs,
        out_shape=jax.ShapeDtypeStruct((s, D_FF), MXU_DTYPE),
        compiler_params=_params(("arbitrary", "arbitrary")),
        name="moe_gmm1",
    )(blk_e, blk_flags, xs, w1, w1, b1r, b1r)


def _gmm2_kernel(be_ref, bf_ref, h_ref, w_ref, b_ref, o_ref, wb_ref):
    del be_ref
    i = pl.program_id(1)
    flags = bf_ref[i]

    @pl.when((flags & 2) != 0)
    def _():
        wb_ref[...] = w_ref[...].astype(wb_ref.dtype)

    @pl.when((flags & 1) != 0)
    def _():
        o_ref[...] = _dot(h_ref[...], wb_ref[...]) + b_ref[...]

    @pl.when((flags & 1) == 0)
    def _():
        o_ref[...] = jnp.zeros_like(o_ref)


def gmm2(blk_e, blk_flags, hs, w2, b2, layer):
    s, f = hs.shape
    d = w2.shape[3]
    nblk = s // MOE_ROWS
    tn = 512
    nj = d // tn
    gs = pltpu.PrefetchScalarGridSpec(
        num_scalar_prefetch=2, grid=(nj, nblk),
        in_specs=[pl.BlockSpec((MOE_ROWS, f), lambda j, i, be, bf: (i, 0)),
                  pl.BlockSpec((None, None, f, tn), lambda j, i, be, bf: (layer, be[i], 0, j)),
                  pl.BlockSpec((None, None, 1, tn), lambda j, i, be, bf: (layer, be[i], 0, j))],
        out_specs=pl.BlockSpec((MOE_ROWS, tn), lambda j, i, be, bf: (i, j)),
        scratch_shapes=[pltpu.VMEM((f, tn), MXU_DTYPE)])
    return pl.pallas_call(
        _gmm2_kernel, grid_spec=gs,
        out_shape=jax.ShapeDtypeStruct((s, d), F32),
        compiler_params=_params(("arbitrary", "arbitrary")),
        name="moe_gmm2",
    )(blk_e, blk_flags, hs, w2, b2.reshape(DEPTH, N_EXPERTS, 1, d))


def _combine_kernel(dcur_ref, dnxt_ref, gate_ref, h_ref, g_ref, b_ref, ys_ref, o_ref, ob_ref, buf_ref, sem):
    i = pl.program_id(0)
    n_steps = pl.num_programs(0)
    n = h_ref.shape[0]
    slot = lax.rem(i, 2)

    def issue(d_ref, s):
        def body(r, carry):
            for k in range(TOP_K):
                d = d_ref[0, r * TOP_K + k]
                _row_copy(ys_ref.at[pl.ds(d, 1), :], buf_ref.at[s, k, pl.ds(r, 1), :], sem.at[s]).start()
            return carry
        lax.fori_loop(0, n, body, 0)

    @pl.when(i == 0)
    def _():
        issue(dcur_ref, 0)

    @pl.when(i + 1 < n_steps)
    def _():
        issue(dnxt_ref, 1 - slot)

    def drain(r, carry):
        for k in range(TOP_K):
            _row_copy(ys_ref.at[pl.ds(0, 1), :], buf_ref.at[slot, k, pl.ds(0, 1), :], sem.at[slot]).wait()
        return carry

    lax.fori_loop(0, n, drain, 0)
    gate = gate_ref[...]
    moe = gate[:, 0:1] * buf_ref[slot, 0]
    for k in range(1, TOP_K):
        moe = moe + gate[:, k:k + 1] * buf_ref[slot, k]
    y = _layer_norm(ALPHA * h_ref[...] + moe, g_ref[...], b_ref[...])
    o_ref[...] = y
    ob_ref[...] = y.astype(ob_ref.dtype)


def combine_ln(dest3, gate, h, ys, g, b):
    n, d = h.shape
    nt = n // MOE_TOK
    smem = lambda f: pl.BlockSpec((None, 1, MOE_TOK * TOP_K), f, memory_space=pltpu.SMEM)
    return pl.pallas_call(
        _combine_kernel,
        grid=(nt,),
        in_specs=[smem(lambda i: (i, 0, 0)),
                  smem(lambda i: (jnp.minimum(i + 1, nt - 1), 0, 0)),
                  pl.BlockSpec((MOE_TOK, 128), lambda i: (i, 0)),
                  pl.BlockSpec((MOE_TOK, d), lambda i: (i, 0)),
                  pl.BlockSpec((1, d), lambda i: (0, 0)),
                  pl.BlockSpec((1, d), lambda i: (0, 0)),
                  pl.BlockSpec(memory_space=pl.ANY)],
        out_specs=[pl.BlockSpec((MOE_TOK, d), lambda i: (i, 0)),
                   pl.BlockSpec((MOE_TOK, d), lambda i: (i, 0))],
        out_shape=[jax.ShapeDtypeStruct((n, d), F32), jax.ShapeDtypeStruct((n, d), MXU_DTYPE)],
        scratch_shapes=[pltpu.VMEM((2, TOP_K, MOE_TOK, d), F32), pltpu.SemaphoreType.DMA((2,))],
        compiler_params=_params(("arbitrary",)),
        name="moe_combine",
    )(dest3, dest3, gate, h, g.reshape(1, d), b.reshape(1, d), ys)


def moe_ln(h, hb, lw):
    n, d = h.shape
    idx, gate, cnt = router(hb, lw["w_router"], lw["b_router"])
    counts = cnt[0, :N_EXPERTS].astype(I32)
    padded = (counts + MOE_ROWS - 1) // MOE_ROWS * MOE_ROWS
    pad_end = jnp.cumsum(padded)
    pad_start = pad_end - padded
    dest = pad_start[idx[:, :TOP_K]] + idx[:, TOP_K:2 * TOP_K]
    dest3 = dest.reshape(n // MOE_TOK, 1, MOE_TOK * TOP_K)
    nblk = -(-(n * TOP_K) // MOE_ROWS) + N_EXPERTS
    blk_start = jnp.arange(nblk, dtype=I32) * MOE_ROWS
    blk_e = jnp.minimum(jnp.searchsorted(pad_end, blk_start, side="right"), N_EXPERTS - 1).astype(I32)
    valid = (blk_start < pad_end[-1]).astype(I32)
    first = jnp.concatenate([jnp.ones((1,), I32), (blk_e[1:] != blk_e[:-1]).astype(I32)])
    blk_flags = valid + 2 * first
    xs = dispatch(dest3, h, nblk * MOE_ROWS)
    hs = gmm1(blk_e, blk_flags, xs, lw["w1"], lw["b1"], lw["layer"])
    ys = gmm2(blk_e, blk_flags, hs, lw["w2"], lw["b2"], lw["layer"])
    return combine_ln(dest3, gate, h, ys, *lw["ln3"])


def _rows8(x):
    r = x.shape[0]
    return x if r == 8 else jnp.concatenate([x, jnp.zeros((8 - r, x.shape[1]), x.dtype)], axis=0)


def _paged_attend(lgs, lg_new, vs, v_new, exact=False, v_t=False):
    r = lg_new.shape[0]
    rnd = (lambda x: x) if exact else _rounded
    if v_t:
        pv = _dot_nt_exact if exact else _dot_nt
    else:
        pv = _dot_exact if exact else _dot
    m = lg_new
    for lg in lgs:
        m = jnp.maximum(m, jnp.max(lg, axis=-1, keepdims=True))
    p_new = jnp.exp(lg_new - m)
    ps = [jnp.exp(lg - m) for lg in lgs]
    tot = p_new
    for p in ps:
        tot = tot + jnp.sum(p, axis=-1, keepdims=True)
    acc = rnd(p_new / tot) * rnd(v_new)
    for p, v in zip(ps, vs):
        acc = acc + pv(_rows8(p / tot), v)[0:r, :]
    return acc


def _page_specs(shape_tail, layer, n_pages):
    nd = len(shape_tail)
    return [pl.BlockSpec((None, None) + shape_tail,
                         functools.partial(lambda b, pt, pg: (layer, pt[b, pg]) + (0,) * nd, pg=pg))
            for pg in range(n_pages)]


def _dsa_dec_score_kernel(pt_ref, *refs, n_pages):
    del pt_ref
    pages = refs[:n_pages]
    iq_ref, iw_ref, new_ref, o_ref = refs[n_pages:]
    iq = iq_ref[...]
    iw = _rounded(iw_ref[...])
    for pg in range(n_pages):
        s = _dot(iq, pages[pg][2 * HD:2 * HD + IDX_DIM, :])
        o_ref[pg:pg + 1, :] = jnp.sum(_rounded(jnp.maximum(s, 0.0)) * iw, axis=0, keepdims=True)
    s_new = jnp.sum(iq.astype(F32) * _rounded(new_ref[:, 2 * HD:2 * HD + IDX_DIM]), axis=-1, keepdims=True)
    sc_new = jnp.sum(_rounded(jnp.maximum(s_new, 0.0)) * iw, axis=0, keepdims=True)
    o_ref[n_pages:n_pages + 1, :] = jnp.broadcast_to(sc_new, (1, PAGE))


def dsa_dec_scores(page_table, cache_dsa, layer, iq, iw, new_rows):
    nb, n_pages = page_table.shape
    gs = pltpu.PrefetchScalarGridSpec(
        num_scalar_prefetch=1, grid=(nb,),
        in_specs=_page_specs((DSA_ROW, PAGE), layer, n_pages) + [
            pl.BlockSpec((None, IDX_H, IDX_DIM), lambda b, pt: (b, 0, 0)),
            pl.BlockSpec((None, IDX_H, 1), lambda b, pt: (b, 0, 0)),
            pl.BlockSpec((None, 1, DSA_ROW), lambda b, pt: (b, 0, 0))],
        out_specs=pl.BlockSpec((None, n_pages + 1, PAGE), lambda b, pt: (b, 0, 0)))
    return pl.pallas_call(
        functools.partial(_dsa_dec_score_kernel, n_pages=n_pages), grid_spec=gs,
        out_shape=jax.ShapeDtypeStruct((nb, n_pages + 1, PAGE), F32),
        compiler_params=_params(("parallel",)),
        name="dsa_dec_scores",
    )(page_table, *([cache_dsa] * n_pages), iq, iw, new_rows)


def _topk_mask_kernel(s_ref, o_ref, *, n_keys, n_top):
    s = s_ref[...]
    valid = lax.broadcasted_iota(I32, s.shape, 1) < n_keys
    key = jnp.where(valid, _sort_key(s), INT_MIN)
    thr = _kth_largest_key(lambda c: jnp.sum(jnp.where(key >= c, 1.0, 0.0), axis=-1, keepdims=True),
                           n_top, (s.shape[0], 1))
    o_ref[...] = jnp.where(valid, jnp.where(key >= thr, 1.0, 0.0), 0.0)


def topk_mask(scores, n_keys, n_top):
    return pl.pallas_call(
        functools.partial(_topk_mask_kernel, n_keys=n_keys, n_top=n_top),
        out_shape=jax.ShapeDtypeStruct(scores.shape, F32),
        compiler_params=pltpu.CompilerParams(vmem_limit_bytes=V7X_VMEM_LIMIT),
        name="topk_mask",
    )(scores)


def _dsa_dec_attend_kernel(pt_ref, *refs, n_pages):
    del pt_ref
    pages = refs[:n_pages]
    q_ref, mask_ref, new_ref, o_ref = refs[n_pages:]
    q = q_ref[...]
    lgs, vs = [], []
    for pg in range(n_pages):
        lg = _dot(_rows8(q), pages[pg][0:HD, :])[0:DSA_H, :] * HD ** -0.5
        lgs.append(jnp.where(mask_ref[pg:pg + 1, :] > 0.5, lg, NEG_INF))
        vs.append(pages[pg][HD:2 * HD, :])
    lg_new = jnp.sum(q.astype(F32) * _rounded(new_ref[:, 0:HD]), axis=-1, keepdims=True) * HD ** -0.5
    lg_new = jnp.where(mask_ref[n_pages:n_pages + 1, 0:1] > 0.5, lg_new, NEG_INF)
    o_ref[...] = _paged_attend(lgs, lg_new, vs, new_ref[:, HD:2 * HD], v_t=True).astype(o_ref.dtype)


def dsa_dec_attend(page_table, cache_dsa, layer, q, mask, new_rows):
    nb, n_pages = page_table.shape
    gs = pltpu.PrefetchScalarGridSpec(
        num_scalar_prefetch=1, grid=(nb,),
        in_specs=_page_specs((DSA_ROW, PAGE), layer, n_pages) + [
            pl.BlockSpec((None, DSA_H, HD), lambda b, pt: (b, 0, 0)),
            pl.BlockSpec((None, n_pages + 1, PAGE), lambda b, pt: (b, 0, 0)),
            pl.BlockSpec((None, 1, DSA_ROW), lambda b, pt: (b, 0, 0))],
        out_specs=pl.BlockSpec((None, DSA_H, HD), lambda b, pt: (b, 0, 0)))
    return pl.pallas_call(
        functools.partial(_dsa_dec_attend_kernel, n_pages=n_pages), grid_spec=gs,
        out_shape=jax.ShapeDtypeStruct((nb, DSA_H, HD), MXU_DTYPE),
        compiler_params=_params(("parallel",)),
        name="dsa_dec_attend",
    )(page_table, *([cache_dsa] * n_pages), q, mask, new_rows)


def _page_head_copy(cache_ref, layer, page, h, buf_ref, slot, pg, sem_ref):
    return pltpu.make_async_copy(cache_ref.at[layer, page, :, h, :], buf_ref.at[slot, pg, h], sem_ref.at[slot])


def _double_buffered_pages(pt_ref, cache_ref, layer, buf_ref, sem_ref, n_pages, n_heads):
    b = pl.program_id(0)
    nb = pl.num_programs(0)
    slot = lax.rem(b, 2)

    def fetch(seq, s):
        for pg in range(n_pages):
            page = pt_ref[seq, pg]
            for h in range(n_heads):
                _page_head_copy(cache_ref, layer, page, h, buf_ref, s, pg, sem_ref).start()

    @pl.when(b == 0)
    def _():
        fetch(b, 0)

    @pl.when(b + 1 < nb)
    def _():
        fetch(b + 1, 1 - slot)

    for pg in range(n_pages):
        for h in range(n_heads):
            _page_head_copy(cache_ref, layer, 0, h, buf_ref, slot, pg, sem_ref).wait()
    return slot


def _moba_dec_kernel(pt_ref, cache_ref, q_ref, new_ref, o_ref, buf_ref, sem_ref, *, layer, n_pages, n_sel):
    slot = _double_buffered_pages(pt_ref, cache_ref, layer, buf_ref, sem_ref, n_pages, MOBA_H)
    ppb = MOBA_BLOCK // PAGE
    nblk = n_pages // ppb
    for h in range(MOBA_H):
        q = q_ref[h:h + 1, :]
        ks = [buf_ref[slot, pg, h, :, 0:HD] for pg in range(n_pages)]
        lane = lax.broadcasted_iota(I32, (1, 128), 1)
        gsc = jnp.full((1, 128), NEG_INF, F32)
        for n in range(nblk):
            tot = jnp.sum(ks[n * ppb], axis=0, keepdims=True)
            for j in range(1, ppb):
                tot = tot + jnp.sum(ks[n * ppb + j], axis=0, keepdims=True)
            mean = tot * (1.0 / MOBA_BLOCK)
            gsc = jnp.where(lane == n, jnp.sum(q * mean, axis=-1, keepdims=True), gsc)
        selm = _top_rank_mask(gsc[:, 0:nblk], n_sel)
        k_new = new_ref[:, h * 2 * HD:h * 2 * HD + HD]
        v_new = new_ref[:, h * 2 * HD + HD:(h + 1) * 2 * HD]
        lg_new = jnp.sum(q * k_new, axis=-1, keepdims=True) * HD ** -0.5
        lgs = []
        m = lg_new
        for pg in range(n_pages):
            lg = jnp.sum(ks[pg] * q, axis=-1, keepdims=True) * HD ** -0.5
            lg = jnp.where(selm[:, pg // ppb:pg // ppb + 1] > 0.5, lg, NEG_INF)
            lgs.append(lg)
            m = jnp.maximum(m, jnp.max(lg, axis=0, keepdims=True))
        p_new = jnp.exp(lg_new - m)
        ps = [jnp.exp(lg - m) for lg in lgs]
        tot = p_new
        for pcol in ps:
            tot = tot + jnp.sum(pcol, axis=0, keepdims=True)
        acc = jnp.zeros((PAGE, HD), F32)
        for pg in range(n_pages):
            acc = acc + _rounded(buf_ref[slot, pg, h, :, HD:2 * HD]) * _rounded(ps[pg] / tot)
        out = jnp.sum(acc, axis=0, keepdims=True) + _rounded(p_new / tot) * _rounded(v_new)
        o_ref[h:h + 1, :] = out.astype(o_ref.dtype)


def moba_dec(page_table, cache_moba, layer, q, new_rows):
    nb, n_pages = page_table.shape
    n_keys = n_pages * PAGE + 1
    n_sel = min(MOBA_TOPK, (n_keys - 1) // MOBA_BLOCK)
    gs = pltpu.PrefetchScalarGridSpec(
        num_scalar_prefetch=1, grid=(nb,),
        in_specs=[pl.BlockSpec(memory_space=pl.ANY),
                  pl.BlockSpec((None, MOBA_H, HD), lambda b, pt: (b, 0, 0)),
                  pl.BlockSpec((None, 1, MOBA_H * 2 * HD), lambda b, pt: (b, 0, 0))],
        out_specs=pl.BlockSpec((None, MOBA_H, HD), lambda b, pt: (b, 0, 0)),
        scratch_shapes=[pltpu.VMEM((2, n_pages, MOBA_H, PAGE, 2 * HD), F32), pltpu.SemaphoreType.DMA((2,))])
    return pl.pallas_call(
        functools.partial(_moba_dec_kernel, layer=layer, n_pages=n_pages, n_sel=n_sel), grid_spec=gs,
        out_shape=jax.ShapeDtypeStruct((nb, MOBA_H, HD), MXU_DTYPE),
        compiler_params=_params(("arbitrary",)),
        name="moba_dec",
    )(page_table, cache_moba, q, new_rows)


def _fox_dec_kernel(pt_ref, cache_ref, q_ref, qt_ref, new_ref, post_ref, ot_ref, buf_ref, sem_ref, *,
                    layer, n_pages):
    slot = _double_buffered_pages(pt_ref, cache_ref, layer, buf_ref, sem_ref, n_pages, FOX_H)
    after = jnp.where(lax.broadcasted_iota(I32, (PAGE, PAGE), 0) > lax.broadcasted_iota(I32, (PAGE, PAGE), 1),
                      1.0, 0.0)
    diag = lax.broadcasted_iota(I32, (HD, HD), 0) == lax.broadcasted_iota(I32, (HD, HD), 1)
    lf_all = jnp.concatenate([buf_ref[slot, pg, h, 2 * HD:2 * HD + 1, :]
                              for pg in range(n_pages) for h in range(FOX_H)], axis=0)
    within_all = _dot_exact(lf_all, after)
    tot_all = jnp.sum(lf_all, axis=-1, keepdims=True)
    for h in range(FOX_H):
        q_col = qt_ref[:, h:h + 1]
        later = post_ref[:, SM_FF + h:SM_FF + h + 1]
        k_new = new_ref[:, h * 2 * HD:h * 2 * HD + HD]
        v_new = new_ref[:, h * 2 * HD + HD:(h + 1) * 2 * HD]
        lg_new = jnp.sum(q_ref[h:h + 1, :] * k_new, axis=-1, keepdims=True) * HD ** -0.5
        lgs = [None] * n_pages
        m = lg_new
        for pg in range(n_pages - 1, -1, -1):
            i = pg * FOX_H + h
            lg = jnp.sum(buf_ref[slot, pg, h, 0:HD, :] * q_col, axis=0, keepdims=True) * HD ** -0.5
            lgs[pg] = lg + (within_all[i:i + 1, :] + later)
            later = later + tot_all[i:i + 1, :]
            m = jnp.maximum(m, jnp.max(lgs[pg], axis=-1, keepdims=True))
        p_new = jnp.exp(lg_new - m)
        ps = [jnp.exp(lg - m) for lg in lgs]
        tot = p_new
        for prow in ps:
            tot = tot + jnp.sum(prow, axis=-1, keepdims=True)
        acc = jnp.zeros((HD, PAGE), F32)
        for pg in range(n_pages):
            acc = acc + buf_ref[slot, pg, h, HD:2 * HD, :] * (ps[pg] / tot)
        v_new_col = jnp.sum(jnp.where(diag, v_new, 0.0), axis=1, keepdims=True)
        ot_ref[:, h:h + 1] = jnp.sum(acc, axis=-1, keepdims=True) + v_new_col * (p_new / tot)


def fox_dec(page_table, cache_fox_t, layer, q, new_kv, post):
    nb, n_pages = page_table.shape
    gs = pltpu.PrefetchScalarGridSpec(
        num_scalar_prefetch=1, grid=(nb,),
        in_specs=[pl.BlockSpec(memory_space=pl.ANY),
                  pl.BlockSpec((None, FOX_H, HD), lambda b, pt: (b, 0, 0)),
                  pl.BlockSpec((None, HD, FOX_H), lambda b, pt: (b, 0, 0)),
                  pl.BlockSpec((None, 1, FOX_H * 2 * HD), lambda b, pt: (b, 0, 0)),
                  pl.BlockSpec((None, 1, 128), lambda b, pt: (b, 0, 0))],
        out_specs=pl.BlockSpec((None, HD, FOX_H), lambda b, pt: (b, 0, 0)),
        scratch_shapes=[pltpu.VMEM((2, n_pages, FOX_H, 2 * HD + 1, PAGE), F32), pltpu.SemaphoreType.DMA((2,))])
    out_t = pl.pallas_call(
        functools.partial(_fox_dec_kernel, layer=layer, n_pages=n_pages), grid_spec=gs,
        out_shape=jax.ShapeDtypeStruct((nb, HD, FOX_H), F32),
        compiler_params=_params(("arbitrary",)),
        name="fox_dec",
    )(page_table, cache_fox_t, q, jnp.transpose(q, (0, 2, 1)), new_kv, post)
    return jnp.transpose(out_t, (0, 2, 1)).astype(MXU_DTYPE)


def _ssd_dec_kernel(st_ref, cs_ref, xbc_ref, z_ref, post_ref, cw_ref, cb_ref, hp_ref, ng_ref,
                    o_ref, sto_ref, cso_ref):
    cs = cs_ref[...]
    xn = xbc_ref[...]
    cw = cw_ref[...]
    conv = cw[0:1, :] * cs[0:1, :]
    for j in range(1, SSM_CONV - 1):
        conv = conv + cw[j:j + 1, :] * cs[j:j + 1, :]
    conv = conv + cw[SSM_CONV - 1:SSM_CONV, :] * xn + cb_ref[...]
    cso_ref[0:SSM_CONV - 2, :] = cs[1:SSM_CONV - 1, :]
    cso_ref[SSM_CONV - 2:SSM_CONV - 1, :] = xn
    u = _silu(conv)
    post = post_ref[...]
    a_row = -jnp.exp(hp_ref[1:2, :])
    diag = lax.broadcasted_iota(I32, (SSM_P, SSM_P), 0) == lax.broadcasted_iota(I32, (SSM_P, SSM_P), 1)
    ys = []
    for h in range(SSM_H):
        g = h // (SSM_H // SSM_G)
        bg = u[:, SSM_DI + g * SSM_N:SSM_DI + (g + 1) * SSM_N]
        cg = u[:, SSM_DI + SSM_G * SSM_N + g * SSM_N:SSM_DI + SSM_G * SSM_N + (g + 1) * SSM_N]
        dt = post[:, SM_DT + h:SM_DT + h + 1]
        adt = dt * a_row[:, SM_DT + h:SM_DT + h + 1]
        x_h = u[:, h * SSM_P:(h + 1) * SSM_P]
        xdt = x_h * dt
        y_diag = xdt * jnp.sum(cg * bg, axis=-1, keepdims=True)
        xdt_col = jnp.sum(jnp.where(diag, xdt, 0.0), axis=1, keepdims=True)
        h0 = _rounded(st_ref[h])
        sto_ref[h] = _rounded(jnp.exp(adt)) * h0 + _rounded(xdt_col * bg)
        y_col = jnp.sum(h0 * _rounded(cg), axis=1, keepdims=True)
        y_off = jnp.exp(adt) * jnp.sum(jnp.where(diag, y_col, 0.0), axis=0, keepdims=True)
        ys.append(y_diag + y_off + x_h * hp_ref[2:3, SM_DT + h:SM_DT + h + 1])
    y = jnp.concatenate(ys, axis=1)
    o_ref[...] = _gated_group_norm(y, z_ref[...], ng_ref[...]).astype(o_ref.dtype)


def ssd_dec(state_ssm, state_conv, layer, xbc, z, post, conv_w, conv_b, head_params, norm_g):
    nb = xbc.shape[0]
    row = lambda w: pl.BlockSpec((None, 1, w), lambda b: (b, 0, 0))
    const = lambda r, w: pl.BlockSpec((r, w), lambda b: (0, 0))
    return pl.pallas_call(
        _ssd_dec_kernel,
        grid=(nb,),
        in_specs=[pl.BlockSpec((None, None, SSM_H, SSM_P, SSM_N), lambda b: (layer, b, 0, 0, 0)),
                  pl.BlockSpec((None, None, SSM_CONV - 1, CONV_DIM), lambda b: (layer, b, 0, 0)),
                  row(CONV_DIM), row(SSM_DI), row(128),
                  const(SSM_CONV, CONV_DIM), const(1, CONV_DIM), const(8, 128), const(1, SSM_DI)],
        out_specs=[row(SSM_DI),
                   pl.BlockSpec((None, SSM_H, SSM_P, SSM_N), lambda b: (b, 0, 0, 0)),
                   pl.BlockSpec((None, SSM_CONV - 1, CONV_DIM), lambda b: (b, 0, 0))],
        out_shape=[jax.ShapeDtypeStruct((nb, 1, SSM_DI), MXU_DTYPE),
                   jax.ShapeDtypeStruct((nb, SSM_H, SSM_P, SSM_N), F32),
                   jax.ShapeDtypeStruct((nb, SSM_CONV - 1, CONV_DIM), F32)],
        compiler_params=_params(("parallel",)),
        name="ssd_dec",
    )(state_ssm, state_conv, xbc.reshape(nb, 1, CONV_DIM), z.reshape(nb, 1, SSM_DI), post.reshape(nb, 1, 128),
      conv_w, conv_b.reshape(1, CONV_DIM), head_params, norm_g.reshape(1, SSM_DI))


def _mem_dec_kernel(q_ref, kv_ref, o_ref):
    for h in range(MEM_H):
        q = q_ref[h:h + 1, :]
        lg = _dot_nt(_rows8(q), kv_ref[:, h, 0:HD])[0:1, :] * HD ** -0.5
        m = jnp.max(lg, axis=-1, keepdims=True)
        p = jnp.exp(lg - m)
        o = _dot(_rows8(p / jnp.sum(p, axis=-1, keepdims=True)), kv_ref[:, h, HD:2 * HD])[0:1, :]
        o_ref[h:h + 1, :] = o.astype(o_ref.dtype)


def mem_dec(mq, cache_mem, layer):
    nb = mq.shape[0]
    mem_len = cache_mem.shape[2]
    return pl.pallas_call(
        _mem_dec_kernel,
        grid=(nb,),
        in_specs=[pl.BlockSpec((None, MEM_H, HD), lambda b: (b, 0, 0)),
                  pl.BlockSpec((None, None, mem_len, MEM_H, 2 * HD), lambda b: (layer, b, 0, 0, 0))],
        out_specs=pl.BlockSpec((None, MEM_H, HD), lambda b: (b, 0, 0)),
        out_shape=jax.ShapeDtypeStruct((nb, MEM_H, HD), MXU_DTYPE),
        compiler_params=_params(("parallel",)),
        name="mem_dec",
    )(mq, cache_mem)


def mixer_sample(xb, lw, layer, p):
    nb = xb.shape[0]
    page_table = p["page_table"]
    n_pages = page_table.shape[1]
    n_keys = n_pages * PAGE + 1
    cache_dsa_t = jnp.transpose(p["cache_dsa"], (0, 1, 3, 2))
    cache_fox_t = jnp.transpose(p["cache_fox"], (0, 1, 4, 3, 2))
    pr = in_proj(xb, lw, F32)
    post, _ = small_post(pr["small"], lw["small_bias"], 1, nb)
    q_all = pr["q_all"]
    q_mxu = q_all[:, :Q_MOBA].astype(MXU_DTYPE)
    dsa_new = pr["dsa_row"].reshape(nb, 1, DSA_ROW)
    scores = dsa_dec_scores(page_table, cache_dsa_t, layer,
                            q_mxu[:, Q_IDX:Q_IDX + IDX_H * IDX_DIM].reshape(nb, IDX_H, IDX_DIM),
                            post[:, SM_IW:SM_IW + IDX_H].reshape(nb, IDX_H, 1), dsa_new)
    mask = topk_mask(scores.reshape(nb, (n_pages + 1) * PAGE), n_keys, min(DSA_TOPK, n_keys // 4))
    o_dsa = dsa_dec_attend(page_table, cache_dsa_t, layer, q_mxu[:, Q_DSA:Q_DSA + DSA_H * HD].reshape(nb, DSA_H, HD),
                           mask.reshape(nb, n_pages + 1, PAGE), dsa_new)
    o_moba = moba_dec(page_table, p["cache_moba"], layer, q_all[:, Q_MOBA:Q_MOBA + MOBA_H * HD].reshape(nb, MOBA_H, HD),
                      pr["moba_row"].reshape(nb, 1, MOBA_H * 2 * HD))
    o_fox = fox_dec(page_table, cache_fox_t, layer, q_all[:, Q_FOX:Q_FOX + FOX_H * HD].reshape(nb, FOX_H, HD),
                    pr["fox_kv"].reshape(nb, 1, FOX_H * 2 * HD), post.reshape(nb, 1, 128))
    o_ssm, ssm_new, conv_new = ssd_dec(p["state_ssm"], p["state_conv"], layer, pr["xbc"], pr["z"], post,
                                       lw["conv_w"], lw["conv_b"], lw["head_params"], lw["norm_g"])
    logf = post[:, SM_FF:SM_FF + FOX_H].reshape(nb, 1, FOX_H, 1)
    rows = (dsa_new,
            pr["moba_row"].reshape(nb, 1, MOBA_H, 2 * HD),
            jnp.concatenate([pr["fox_kv"].reshape(nb, 1, FOX_H, 2 * HD), logf], axis=-1),
            ssm_new, conv_new)
    outs = (o_dsa.reshape(nb, DSA_H * HD), o_moba.reshape(nb, MOBA_H * HD), o_ssm.reshape(nb, SSM_DI),
            o_fox.reshape(nb, FOX_H * HD))
    return outs, rows


def kernel(x_prompt, x_sample, mem_prompt, cache_dsa, cache_moba, cache_fox, cache_mem, state_ssm, state_conv,
           page_table, w_in, fox_b_f, conv_w, conv_b, dt_bias, a_log, d_skip, ssm_norm_g, w_br_dsa, w_br_moba,
           w_br_ssm, w_br_fox, w_gate, b_gate, w_o, ln1_g, ln1_b, w_mq, w_mkv, w_mo, ln2_g, ln2_b, w_router,
           b_router, w1, b1, w2, b2, ln3_g, ln3_b):
    p = dict(cache_dsa=cache_dsa, cache_moba=cache_moba, cache_fox=cache_fox, cache_mem=cache_mem,
             state_ssm=state_ssm, state_conv=state_conv, page_table=page_table, w_in=w_in, fox_b_f=fox_b_f,
             conv_w=conv_w, conv_b=conv_b, dt_bias=dt_bias, a_log=a_log, d_skip=d_skip, ssm_norm_g=ssm_norm_g,
             w_br_dsa=w_br_dsa, w_br_moba=w_br_moba, w_br_ssm=w_br_ssm, w_br_fox=w_br_fox, w_gate=w_gate,
             b_gate=b_gate, w_o=w_o, ln1_g=ln1_g, ln1_b=ln1_b, w_mq=w_mq, w_mkv=w_mkv, w_mo=w_mo, ln2_g=ln2_g,
             ln2_b=ln2_b, w_router=w_router, b_router=b_router, w1=w1, b1=b1, w2=w2, b2=b2, ln3_g=ln3_g,
             ln3_b=ln3_b)
    nb, t, d = x_prompt.shape
    ns = x_sample.shape[0]
    mem_len = mem_prompt.shape[1]
    n_p = nb * t
    xp, xs = x_prompt.reshape(n_p, d), x_sample.reshape(ns, d)
    xpb, xsb = xp.astype(MXU_DTYPE), xs.astype(MXU_DTYPE)
    memb = mem_prompt.reshape(nb * mem_len, d).astype(MXU_DTYPE)
    st_p, st_s = [], []
    for l in range(DEPTH):
        lw = prep_layer(l, p)
        outs_p, rows_p = mixer_prompt(xpb, lw, nb, t)
        outs_s, rows_s = mixer_sample(xsb, lw, l, p)
        h1p, h1pb = mm_ln(gate_merge(xpb, outs_p, lw["w_gate"], lw["b_gate"], lw["projs"]), lw["w_o"], xp, *lw["ln1"])
        h1s, h1sb = mm_ln(gate_merge(xsb, outs_s, lw["w_gate"], lw["b_gate"], lw["projs"]), lw["w_o"], xs, *lw["ln1"])
        mem_kv = mm(memb, lw["w_mkv"])
        om_p = mem_prompt_attn(mm(h1pb, lw["w_mq"], MXU_DTYPE), mem_kv, nb, t, mem_len)
        om_s = mem_dec(mm(h1sb, lw["w_mq"], MXU_DTYPE).reshape(ns, MEM_H, HD), cache_mem, l).reshape(ns, MEM_H * HD)
        h2p, h2pb = mm_ln(om_p, lw["w_mo"], h1p, *lw["ln2"])
        h2s, h2sb = mm_ln(om_s, lw["w_mo"], h1s, *lw["ln2"])
        y, yb = moe_ln(jnp.concatenate([h2p, h2s], axis=0), jnp.concatenate([h2pb, h2sb], axis=0), lw)
        xp, xs, xpb, xsb = y[:n_p], y[n_p:], yb[:n_p], yb[n_p:]
        st_p.append(rows_p + (mem_kv.reshape(nb, mem_len, MEM_H, 2 * HD),))
        st_s.append(rows_s)
    sp = [jnp.stack(z) for z in zip(*st_p)]
    ss = [jnp.stack(z) for z in zip(*st_s)]
    return (xp.reshape(nb, t, d), xs.reshape(ns, 1, d), sp[0], ss[0], sp[1], ss[1], sp[2], ss[2], sp[3], ss[3],
            sp[4], ss[4], sp[5])
```

```python
import functools

import jax
import jax.numpy as jnp
from jax import lax
from jax.experimental import pallas as pl
from jax.experimental.pallas import tpu as pltpu

F32 = jnp.float32
I32 = jnp.int32
MXU_DTYPE = jnp.bfloat16
V7X_VMEM_LIMIT = 56 * 1024 * 1024

D_MODEL = 2048
DEPTH = 2
PAGE = 128
HD = 128
DSA_H, IDX_H, IDX_DIM, DSA_TOPK = 4, 16, 64, 256
DSA_ROW = 2 * HD + IDX_DIM
MOBA_H, MOBA_BLOCK, MOBA_TOPK = 4, 256, 3
SSM_H, SSM_P, SSM_G, SSM_N, SSM_CONV = 16, 64, 2, 128, 4
SSM_DI = SSM_H * SSM_P
CONV_DIM = SSM_DI + 2 * SSM_G * SSM_N
SSD_CHUNK = 256
FOX_H = 4
MEM_H = 4
N_EXPERTS, TOP_K, D_FF = 32, 4, 2048
SWIGLU_LIMIT, SWIGLU_ALPHA = 7.0, 1.702
ALPHA = (2 * DEPTH) ** 0.25
LN_EPS = 1e-5
RMS_EPS = 1e-5
IN_WIDTHS = (DSA_H * HD, 2 * HD, IDX_H * IDX_DIM, IDX_DIM, IDX_H, 3 * MOBA_H * HD,
             SSM_DI, CONV_DIM, SSM_H, 3 * FOX_H * HD, FOX_H)
SM_IW, SM_DT, SM_FF = 0, IDX_H, IDX_H + SSM_H
Q_IDX, Q_DSA, Q_MOBA, Q_FOX = 0, 1024, 1536, 2048
Q_WIDTH = 2560
MOE_ROWS = 512
NEG_INF = float("-inf")
INT_MIN = -2 ** 31


def _params(sem):
    return pltpu.CompilerParams(dimension_semantics=sem, vmem_limit_bytes=V7X_VMEM_LIMIT)


def _dot(a, b):
    return jnp.dot(a.astype(MXU_DTYPE), b.astype(MXU_DTYPE), preferred_element_type=F32)


def _dot_nt(a, b):
    return lax.dot_general(a.astype(MXU_DTYPE), b.astype(MXU_DTYPE), (((1,), (1,)), ((), ())),
                           preferred_element_type=F32)


def _dot_exact(a, b):
    return jnp.dot(a, b, preferred_element_type=F32, precision=lax.Precision.HIGHEST)


def _dot_nt_exact(a, b):
    return lax.dot_general(a, b, (((1,), (1,)), ((), ())), preferred_element_type=F32,
                           precision=lax.Precision.HIGHEST)


def _rounded(x):
    return x.astype(MXU_DTYPE).astype(F32)


def _sigmoid(x):
    return 1.0 / (1.0 + jnp.exp(-x))


def _softplus(x):
    return jnp.maximum(x, 0.0) + jnp.log1p(jnp.exp(-jnp.abs(x)))


def _log_sigmoid(x):
    return jnp.minimum(x, 0.0) - jnp.log1p(jnp.exp(-jnp.abs(x)))


def _silu(x):
    return x * _sigmoid(x)


def _pick(n, cands):
    for c in cands:
        if n % c == 0:
            return c
    return n


def _mm_kernel(x_ref, w_ref, o_ref):
    o_ref[...] = _dot(x_ref[...], w_ref[...]).astype(o_ref.dtype)


def mm(x, w, out_dtype=F32):
    m, k = x.shape
    n = w.shape[1]
    tm = _pick(m, (1024, 512, 256, 128))
    tn = _pick(n, (512, 384, 256, 128)) if n % 128 == 0 else n
    return pl.pallas_call(
        _mm_kernel,
        grid=(m // tm, n // tn),
        in_specs=[pl.BlockSpec((tm, k), lambda i, j: (i, 0)),
                  pl.BlockSpec((k, tn), lambda i, j: (0, j))],
        out_specs=pl.BlockSpec((tm, tn), lambda i, j: (i, j)),
        out_shape=jax.ShapeDtypeStruct((m, n), out_dtype),
        compiler_params=_params(("parallel", "arbitrary")),
        name="mm",
    )(x, w)


def _layer_norm(z, g, b):
    mu = jnp.mean(z, axis=-1, keepdims=True)
    zc = z - mu
    var = jnp.mean(zc * zc, axis=-1, keepdims=True)
    return zc * lax.rsqrt(var + LN_EPS) * g + b


def _mm_ln_kernel(a_ref, w_ref, r_ref, g_ref, b_ref, o_ref, ob_ref):
    z = ALPHA * r_ref[...] + _dot(a_ref[...], w_ref[...])
    y = _layer_norm(z, g_ref[...], b_ref[...])
    o_ref[...] = y
    ob_ref[...] = y.astype(ob_ref.dtype)


def mm_ln(a, w, resid, g, b):
    m, k = a.shape
    d = w.shape[1]
    tm = _pick(m, (512, 256, 128))
    return pl.pallas_call(
        _mm_ln_kernel,
        grid=(m // tm,),
        in_specs=[pl.BlockSpec((tm, k), lambda i: (i, 0)),
                  pl.BlockSpec((k, d), lambda i: (0, 0)),
                  pl.BlockSpec((tm, d), lambda i: (i, 0)),
                  pl.BlockSpec((1, d), lambda i: (0, 0)),
                  pl.BlockSpec((1, d), lambda i: (0, 0))],
        out_specs=[pl.BlockSpec((tm, d), lambda i: (i, 0)),
                   pl.BlockSpec((tm, d), lambda i: (i, 0))],
        out_shape=[jax.ShapeDtypeStruct((m, d), F32), jax.ShapeDtypeStruct((m, d), MXU_DTYPE)],
        compiler_params=_params(("parallel",)),
        name="mm_ln",
    )(a, w, resid, g.reshape(1, d), b.reshape(1, d))


def _gate_merge_kernel(x_ref, o0_ref, o1_ref, o2_ref, o3_ref, wg0_ref, wg1_ref, wg2_ref, wg3_ref,
                       bg0_ref, bg1_ref, bg2_ref, bg3_ref, p0_ref, p1_ref, p2_ref, p3_ref, m_ref):
    x = x_ref[...]
    acc = None
    for o_ref, wg_ref, bg_ref, p_ref in ((o0_ref, wg0_ref, bg0_ref, p0_ref), (o1_ref, wg1_ref, bg1_ref, p1_ref),
                                         (o2_ref, wg2_ref, bg2_ref, p2_ref), (o3_ref, wg3_ref, bg3_ref, p3_ref)):
        g = _sigmoid(_dot(x, wg_ref[...]) + bg_ref[...])
        t = g * _dot(o_ref[...], p_ref[...])
        acc = t if acc is None else acc + t
    m_ref[...] = acc.astype(m_ref.dtype)


def gate_merge(xb, outs, w_gate, b_gate, projs):
    m, d = xb.shape
    tm = _pick(m, (1024, 512, 256, 128))
    tn = 256
    nj = d // tn
    in_specs = [pl.BlockSpec((tm, d), lambda i, j: (i, 0))]
    in_specs += [pl.BlockSpec((tm, o.shape[1]), lambda i, j: (i, 0)) for o in outs]
    in_specs += [pl.BlockSpec((d, tn), functools.partial(lambda i, j, br: (0, br * nj + j), br=br)) for br in range(4)]
    in_specs += [pl.BlockSpec((1, tn), functools.partial(lambda i, j, br: (0, br * nj + j), br=br)) for br in range(4)]
    in_specs += [pl.BlockSpec((p.shape[0], tn), lambda i, j: (0, j)) for p in projs]
    bg = b_gate.reshape(1, 4 * d)
    return pl.pallas_call(
        _gate_merge_kernel,
        grid=(m // tm, nj),
        in_specs=in_specs,
        out_specs=pl.BlockSpec((tm, tn), lambda i, j: (i, j)),
        out_shape=jax.ShapeDtypeStruct((m, d), MXU_DTYPE),
        compiler_params=_params(("parallel", "arbitrary")),
        name="gate_merge",
    )(xb, *outs, w_gate, w_gate, w_gate, w_gate, bg, bg, bg, bg, *projs)


def _tri_incl(n):
    row = lax.broadcasted_iota(I32, (n, n), 0)
    col = lax.broadcasted_iota(I32, (n, n), 1)
    return col <= row


def _small_post_kernel(s_ref, bias_ref, post_ref, fcum_ref, carry_ref):
    c = pl.program_id(1)

    @pl.when(c == 0)
    def _():
        carry_ref[...] = jnp.zeros_like(carry_ref)

    raw = s_ref[...]
    z = raw + bias_ref[...]
    lane = lax.broadcasted_iota(I32, z.shape, 1)
    is_f = (lane >= SM_FF) & (lane < SM_FF + FOX_H)
    lf = jnp.where(is_f, _log_sigmoid(z), 0.0)
    post_ref[...] = jnp.where(lane < SM_DT, raw, jnp.where(lane < SM_FF, _softplus(z), lf))
    ck = z.shape[0]
    cs = _dot_exact(jnp.where(_tri_incl(ck), 1.0, 0.0), lf) + carry_ref[...]
    fcum_ref[...] = cs
    carry_ref[...] = cs[ck - 1:ck, :]


def small_post(small, bias_row, nb, t):
    ck = min(t, 256)
    nc = t // ck
    return pl.pallas_call(
        _small_post_kernel,
        grid=(nb, nc),
        in_specs=[pl.BlockSpec((ck, 128), lambda b, c: (b * nc + c, 0)),
                  pl.BlockSpec((1, 128), lambda b, c: (0, 0))],
        out_specs=[pl.BlockSpec((ck, 128), lambda b, c: (b * nc + c, 0)),
                   pl.BlockSpec((ck, 128), lambda b, c: (b * nc + c, 0))],
        out_shape=[jax.ShapeDtypeStruct(small.shape, F32), jax.ShapeDtypeStruct(small.shape, F32)],
        scratch_shapes=[pltpu.VMEM((1, 128), F32)],
        compiler_params=_params(("parallel", "arbitrary")),
        name="small_post",
    )(small, bias_row)


def _softmax_av(lg, v):
    m = jnp.max(lg, axis=-1, keepdims=True)
    p = jnp.exp(lg - m)
    return _dot(p / jnp.sum(p, axis=-1, keepdims=True), v)


def _sort_key(s):
    bits = lax.bitcast_convert_type(s + 0.0, I32)
    return bits ^ (lax.shift_right_arithmetic(bits, 31) & 0x7FFFFFFF)


def _kth_largest_key(count_ge, k, shape):
    kf = float(k)
    t0 = jnp.where(count_ge(jnp.zeros(shape, I32)) >= kf, 0, INT_MIN).astype(I32)

    def body(i, t):
        cand = t | lax.shift_left(jnp.int32(1), 30 - i)
        return jnp.where(count_ge(cand) >= kf, cand, t)

    return lax.fori_loop(0, 31, body, t0)


def _top_rank_mask(gsc, n_sel):
    nb = gsc.shape[1]
    blk = lax.broadcasted_iota(I32, gsc.shape, 1)
    rank = jnp.zeros(gsc.shape, F32)
    for m in range(nb):
        cm = gsc[:, m:m + 1]
        beats = jnp.where(cm > gsc, 1.0, jnp.where(cm == gsc, jnp.where(blk > m, 1.0, 0.0), 0.0))
        rank = rank + beats
    return jnp.where(rank < float(n_sel), jnp.where(gsc > NEG_INF, 1.0, 0.0), 0.0)


def _causal_groups(n_tiles):
    return min(4, n_tiles)


def _for_visible_keys(qi, n_tiles, tile, body):
    n_groups = _causal_groups(n_tiles)
    per = n_tiles // n_groups
    for g in range(n_groups):
        pl.when(qi // per == g)(functools.partial(body, (g + 1) * per * tile))


def _dsa_prompt_kernel(iq_ref, q_ref, sm_ref, row_ref, o_ref, *, n_top):
    qi = pl.program_id(1)
    tq = q_ref.shape[0]
    _for_visible_keys(qi, row_ref.shape[0] // tq, tq,
                      functools.partial(_dsa_prompt_body, iq_ref, q_ref, sm_ref, row_ref, o_ref, qi, n_top))


def _dsa_prompt_body(iq_ref, q_ref, sm_ref, row_ref, o_ref, qi, n_top, t):
    tq = q_ref.shape[0]
    row_ref = row_ref.at[0:t, :]
    kidx = row_ref[:, 2 * HD:2 * HD + IDX_DIM].astype(MXU_DTYPE)
    iq = iq_ref[...]
    iw = _rounded(sm_ref[:, SM_IW:SM_IW + IDX_H])
    score = jnp.zeros((tq, t), F32)
    for h in range(IDX_H):
        s = _dot_nt(iq[:, h * IDX_DIM:(h + 1) * IDX_DIM], kidx)
        score = score + _rounded(jnp.maximum(s, 0.0)) * iw[:, h:h + 1]
    tpos = qi * tq + lax.broadcasted_iota(I32, (tq, t), 0)
    lpos = lax.broadcasted_iota(I32, (tq, t), 1)
    causal = lpos <= tpos
    key = jnp.where(causal, _sort_key(score), INT_MIN)
    thr = _kth_largest_key(lambda c: jnp.sum(jnp.where(key >= c, 1.0, 0.0), axis=-1, keepdims=True),
                           n_top, (tq, 1))
    sel = key >= thr
    kb = row_ref[:, 0:HD].astype(MXU_DTYPE)
    vb = row_ref[:, HD:2 * HD].astype(MXU_DTYPE)
    for h in range(DSA_H):
        lg = _dot_nt(q_ref[:, h * HD:(h + 1) * HD], kb) * HD ** -0.5
        lg = jnp.where(causal, jnp.where(sel, lg, NEG_INF), NEG_INF)
        o_ref[:, h * HD:(h + 1) * HD] = _softmax_av(lg, vb).astype(o_ref.dtype)


def dsa_prompt(q_all, post, dsa_row, nb, t, n_top):
    tq = min(t, 256)
    nq = t // tq
    return pl.pallas_call(
        functools.partial(_dsa_prompt_kernel, n_top=n_top),
        grid=(nb, nq),
        in_specs=[pl.BlockSpec((tq, IDX_H * IDX_DIM), lambda b, i: (b * nq + i, Q_IDX // 1024)),
                  pl.BlockSpec((tq, DSA_H * HD), lambda b, i: (b * nq + i, Q_DSA // 512)),
                  pl.BlockSpec((tq, 128), lambda b, i: (b * nq + i, 0)),
                  pl.BlockSpec((t, DSA_ROW), lambda b, i: (b, 0))],
        out_specs=pl.BlockSpec((tq, DSA_H * HD), lambda b, i: (b * nq + i, 0)),
        out_shape=jax.ShapeDtypeStruct((nb * t, DSA_H * HD), MXU_DTYPE),
        compiler_params=_params(("parallel", "arbitrary")),
        name="dsa_prompt",
    )(q_all, q_all, post, dsa_row)


def _moba_prompt_kernel(q_ref, kv_ref, o_ref, kb_ref, vb_ref, mean_ref, *, n_sel):
    qi = pl.program_id(2)
    t = kv_ref.shape[0]
    nblk = t // MOBA_BLOCK

    @pl.when(qi == 0)
    def _():
        kb_ref[...] = kv_ref[:, 0:HD].astype(MXU_DTYPE)
        vb_ref[...] = kv_ref[:, HD:2 * HD].astype(MXU_DTYPE)
        for n in range(nblk):
            mean_ref[n:n + 1, :] = jnp.mean(kv_ref[n * MOBA_BLOCK:(n + 1) * MOBA_BLOCK, 0:HD], axis=0,
                                            keepdims=True)

    _for_visible_keys(qi, nblk, MOBA_BLOCK,
                      functools.partial(_moba_prompt_body, q_ref, o_ref, kb_ref, vb_ref, mean_ref, qi, n_sel))


def _moba_prompt_body(q_ref, o_ref, kb_ref, vb_ref, mean_ref, qi, n_sel, t):
    nblk = t // MOBA_BLOCK
    q = q_ref[...]
    logits = _dot_nt(q, kb_ref[0:t, :]) * HD ** -0.5
    tq = q.shape[0]
    own_causal = (lax.broadcasted_iota(I32, (tq, MOBA_BLOCK), 1) <= lax.broadcasted_iota(I32, (tq, MOBA_BLOCK), 0))
    if n_sel:
        gsc = _dot_nt(q, mean_ref[...])
        gsc = jnp.where(lax.broadcasted_iota(I32, gsc.shape, 1) < qi, gsc, NEG_INF)
        selm = _top_rank_mask(gsc, n_sel)
    pieces = []
    for n in range(nblk):
        lg = logits[:, n * MOBA_BLOCK:(n + 1) * MOBA_BLOCK]
        piece = jnp.where(qi == n, jnp.where(own_causal, lg, NEG_INF), NEG_INF)
        if n_sel:
            piece = jnp.where(selm[:, n:n + 1] > 0.5, lg, piece)
        pieces.append(piece)
    lg = jnp.concatenate(pieces, axis=1) if nblk > 1 else pieces[0]
    o_ref[...] = _softmax_av(lg, vb_ref[0:t, :]).astype(o_ref.dtype)


def moba_prompt(q_all, moba_row, nb, t, n_sel):
    nq = t // MOBA_BLOCK
    return pl.pallas_call(
        functools.partial(_moba_prompt_kernel, n_sel=n_sel),
        grid=(nb, MOBA_H, nq),
        in_specs=[pl.BlockSpec((MOBA_BLOCK, HD), lambda b, h, i: (b * nq + i, Q_MOBA // HD + h)),
                  pl.BlockSpec((t, 2 * HD), lambda b, h, i: (b, h))],
        out_specs=pl.BlockSpec((MOBA_BLOCK, HD), lambda b, h, i: (b * nq + i, h)),
        out_shape=jax.ShapeDtypeStruct((nb * t, MOBA_H * HD), MXU_DTYPE),
        scratch_shapes=[pltpu.VMEM((t, HD), MXU_DTYPE), pltpu.VMEM((t, HD), MXU_DTYPE),
                        pltpu.VMEM((nq, HD), F32)],
        compiler_params=_params(("parallel", "parallel", "arbitrary")),
        name="moba_prompt",
    )(q_all, moba_row)


def _fox_prompt_kernel(q_ref, kv_ref, fc_ref, fk_ref, o_ref, kb_ref, vb_ref):
    h = pl.program_id(1)
    qi = pl.program_id(2)

    @pl.when(qi == 0)
    def _():
        kb_ref[...] = kv_ref[:, 0:HD].astype(MXU_DTYPE)
        vb_ref[...] = kv_ref[:, HD:2 * HD].astype(MXU_DTYPE)

    tq = q_ref.shape[0]
    _for_visible_keys(qi, kv_ref.shape[0] // tq, tq,
                      functools.partial(_fox_prompt_body, q_ref, fc_ref, fk_ref, o_ref, kb_ref, vb_ref, h, qi))


def _fox_prompt_body(q_ref, fc_ref, fk_ref, o_ref, kb_ref, vb_ref, h, qi, t):
    tq = q_ref.shape[0]
    fc = fc_ref[...]
    fq = jnp.sum(jnp.where(lax.broadcasted_iota(I32, fc.shape, 1) == SM_FF + h, fc, 0.0), axis=-1, keepdims=True)
    lg = _dot_nt(q_ref[...], kb_ref[0:t, :]) * HD ** -0.5
    lg = lg + fq - fk_ref[:, 0:t]
    causal = lax.broadcasted_iota(I32, (tq, t), 1) <= qi * tq + lax.broadcasted_iota(I32, (tq, t), 0)
    lg = jnp.where(causal, lg, NEG_INF)
    o_ref[...] = _softmax_av(lg, vb_ref[0:t, :]).astype(o_ref.dtype)


def fox_prompt(q_all, fox_kv, fcum, fk_rows, nb, t):
    tq = min(t, 256)
    nq = t // tq
    return pl.pallas_call(
        _fox_prompt_kernel,
        grid=(nb, FOX_H, nq),
        in_specs=[pl.BlockSpec((tq, HD), lambda b, h, i: (b * nq + i, Q_FOX // HD + h)),
                  pl.BlockSpec((t, 2 * HD), lambda b, h, i: (b, h)),
                  pl.BlockSpec((tq, 128), lambda b, h, i: (b * nq + i, 0)),
                  pl.BlockSpec((None, 1, t), lambda b, h, i: (b * FOX_H + h, 0, 0))],
        out_specs=pl.BlockSpec((tq, HD), lambda b, h, i: (b * nq + i, h)),
        out_shape=jax.ShapeDtypeStruct((nb * t, FOX_H * HD), MXU_DTYPE),
        scratch_shapes=[pltpu.VMEM((t, HD), MXU_DTYPE), pltpu.VMEM((t, HD), MXU_DTYPE)],
        compiler_params=_params(("parallel", "parallel", "arbitrary")),
        name="fox_prompt",
    )(q_all, fox_kv, fcum, fk_rows)


def _gated_group_norm(y, z, ng):
    y = y * _silu(z)
    gw = SSM_DI // SSM_G
    parts = []
    for g in range(SSM_G):
        yg = y[:, g * gw:(g + 1) * gw]
        parts.append(yg * lax.rsqrt(jnp.mean(yg * yg, axis=-1, keepdims=True) + RMS_EPS))
    return jnp.concatenate(parts, axis=1) * ng


def _chunk_decays(alast_ref, z, nc):
    rows = []
    for j in range(nc):
        e = jnp.zeros((1, 128), F32)
        for m in range(j + 1, nc):
            e = e + jnp.where(m < z, alast_ref[m], 0.0)
        rows.append(jnp.where(j < z, _rounded(jnp.exp(e)), 0.0))
    return rows


def _carried_state(dec, sloc_ref, h):
    acc = None
    for j, d in enumerate(dec):
        t = d[:, SM_DT + h:SM_DT + h + 1] * sloc_ref[j, h].astype(F32)
        acc = t if acc is None else acc + t
    return acc


def _ssd_prompt_kernel(xbc_ref, z_ref, post_ref, cw_ref, cb_ref, hp_ref, ng_ref, o_ref, st_ref,
                       sloc_ref, alast_ref, tail_ref):
    c = pl.program_id(1)
    nc = sloc_ref.shape[0]
    ln = xbc_ref.shape[0]

    @pl.when(c == 0)
    def _():
        sloc_ref[...] = jnp.zeros_like(sloc_ref)
        alast_ref[...] = jnp.zeros_like(alast_ref)
        tail_ref[...] = jnp.zeros_like(tail_ref)

    xin = xbc_ref[...]
    xx = _rounded(jnp.concatenate([tail_ref[...], xin], axis=0))
    cw = _rounded(cw_ref[...])
    conv = cw[0:1, :] * xx[5:5 + ln, :]
    for j in range(1, SSM_CONV):
        conv = conv + cw[j:j + 1, :] * xx[5 + j:5 + j + ln, :]
    conv = conv + cb_ref[...]
    tail_ref[...] = xin[ln - 8:ln, :]
    u = _silu(conv)
    xs = u[:, :SSM_DI]
    xs_t = xs.T
    lane = lax.broadcasted_iota(I32, (ln, 128), 1)
    is_dt = (lane >= SM_DT) & (lane < SM_DT + SSM_H)
    dt = jnp.where(is_dt, post_ref[...], 0.0)
    a_row = -jnp.exp(hp_ref[1:2, :])
    adt = jnp.where(is_dt, dt * a_row, 0.0)
    low = _tri_incl(ln)
    acs = _dot_exact(jnp.where(low, 1.0, 0.0), adt)
    acs_t = acs.T
    dt_t = dt.T
    a_last = acs[ln - 1:ln, :]
    dec = _chunk_decays(alast_ref, c, nc)
    ys = []
    for g in range(SSM_G):
        bg = u[:, SSM_DI + g * SSM_N:SSM_DI + (g + 1) * SSM_N]
        cg = u[:, SSM_DI + SSM_G * SSM_N + g * SSM_N:SSM_DI + SSM_G * SSM_N + (g + 1) * SSM_N]
        cb = _dot_nt(cg, bg)
        for hh in range(SSM_H // SSM_G):
            h = g * (SSM_H // SSM_G) + hh
            col = acs[:, SM_DT + h:SM_DT + h + 1]
            row = acs_t[SM_DT + h:SM_DT + h + 1, :]
            last = a_last[:, SM_DT + h:SM_DT + h + 1]
            x_h = xs[:, h * SSM_P:(h + 1) * SSM_P]
            xdt = x_h * dt[:, SM_DT + h:SM_DT + h + 1]
            scores = cb * jnp.where(low, jnp.exp(col - row), 0.0)
            y = _dot(scores, xdt) + jnp.exp(col) * _dot_nt(cg, _carried_state(dec, sloc_ref, h))
            ys.append(y + x_h * hp_ref[2:3, SM_DT + h:SM_DT + h + 1])
            xdt_t = (xs_t[h * SSM_P:(h + 1) * SSM_P, :] * dt_t[SM_DT + h:SM_DT + h + 1, :]) * jnp.exp(last - row)
            sloc_ref[c, h] = _dot(xdt_t, bg).astype(sloc_ref.dtype)
    alast_ref[c] = a_last
    y = jnp.concatenate(ys, axis=1)
    o_ref[...] = _gated_group_norm(y, z_ref[...], ng_ref[...]).astype(o_ref.dtype)

    @pl.when(c == nc - 1)
    def _():
        dec_end = _chunk_decays(alast_ref, nc, nc)
        for h in range(SSM_H):
            st_ref[h] = _carried_state(dec_end, sloc_ref, h)


def ssd_prompt(xbc, z, post, conv_w, conv_b, head_params, norm_g, nb, t):
    ck = min(t, SSD_CHUNK)
    nc = t // ck
    return pl.pallas_call(
        _ssd_prompt_kernel,
        grid=(nb, nc),
        in_specs=[pl.BlockSpec((ck, CONV_DIM), lambda b, c: (b * nc + c, 0)),
                  pl.BlockSpec((ck, SSM_DI), lambda b, c: (b * nc + c, 0)),
                  pl.BlockSpec((ck, 128), lambda b, c: (b * nc + c, 0)),
                  pl.BlockSpec((SSM_CONV, CONV_DIM), lambda b, c: (0, 0)),
                  pl.BlockSpec((1, CONV_DIM), lambda b, c: (0, 0)),
                  pl.BlockSpec((8, 128), lambda b, c: (0, 0)),
                  pl.BlockSpec((1, SSM_DI), lambda b, c: (0, 0))],
        out_specs=[pl.BlockSpec((ck, SSM_DI), lambda b, c: (b * nc + c, 0)),
                   pl.BlockSpec((None, SSM_H, SSM_P, SSM_N), lambda b, c: (b, 0, 0, 0))],
        out_shape=[jax.ShapeDtypeStruct((nb * t, SSM_DI), MXU_DTYPE),
                   jax.ShapeDtypeStruct((nb, SSM_H, SSM_P, SSM_N), F32)],
        scratch_shapes=[pltpu.VMEM((nc, SSM_H, SSM_P, SSM_N), MXU_DTYPE), pltpu.VMEM((nc, 1, 128), F32),
                        pltpu.VMEM((8, CONV_DIM), F32)],
        compiler_params=_params(("parallel", "arbitrary")),
        name="ssd_prompt",
    )(xbc, z, post, conv_w, conv_b.reshape(1, CONV_DIM), head_params, norm_g.reshape(1, SSM_DI))


def _mem_prompt_kernel(q_ref, kv_ref, o_ref):
    for h in range(MEM_H):
        k = kv_ref[:, h * 2 * HD:h * 2 * HD + HD]
        v = kv_ref[:, h * 2 * HD + HD:(h + 1) * 2 * HD]
        lg = _dot_nt(q_ref[:, h * HD:(h + 1) * HD], k) * HD ** -0.5
        o_ref[:, h * HD:(h + 1) * HD] = _softmax_av(lg, v).astype(o_ref.dtype)


def mem_prompt_attn(mq, mem_kv, nb, t, mem_len):
    tq = _pick(t, (512, 256, 128))
    nq = t // tq
    return pl.pallas_call(
        _mem_prompt_kernel,
        grid=(nb, nq),
        in_specs=[pl.BlockSpec((tq, MEM_H * HD), lambda b, i: (b * nq + i, 0)),
                  pl.BlockSpec((mem_len, MEM_H * 2 * HD), lambda b, i: (b, 0))],
        out_specs=pl.BlockSpec((tq, MEM_H * HD), lambda b, i: (b * nq + i, 0)),
        out_shape=jax.ShapeDtypeStruct((nb * t, MEM_H * HD), MXU_DTYPE),
        compiler_params=_params(("parallel", "arbitrary")),
        name="mem_prompt",
    )(mq, mem_kv)


def _split_cols(w, widths):
    out, o = [], 0
    for wd in widths:
        out.append(w[:, o:o + wd])
        o += wd
    return out


def _interleave_heads(k, v, n_heads):
    pieces = []
    for h in range(n_heads):
        pieces += [k[:, h * HD:(h + 1) * HD], v[:, h * HD:(h + 1) * HD]]
    return jnp.concatenate(pieces, axis=1)


def prep_layer(l, p):
    c = lambda a: a.astype(MXU_DTYPE)
    (dsa_q, dsa_kv, idx_q, idx_k, idx_w, moba_qkv, ssm_z, ssm_xbc, ssm_dt, fox_qkv, fox_f) = _split_cols(
        p["w_in"][l], IN_WIDTHS)
    moba_q, moba_k, moba_v = _split_cols(moba_qkv, (MOBA_H * HD,) * 3)
    fox_q, fox_k, fox_v = _split_cols(fox_qkv, (FOX_H * HD,) * 3)
    pad = jnp.zeros((D_MODEL, 128 - SM_FF - FOX_H), F32)
    small_bias = jnp.zeros((1, 128), F32)
    small_bias = small_bias.at[0, SM_DT:SM_DT + SSM_H].set(p["dt_bias"][l])
    small_bias = small_bias.at[0, SM_FF:SM_FF + FOX_H].set(p["fox_b_f"][l])
    head_params = jnp.zeros((8, 128), F32)
    head_params = head_params.at[1, SM_DT:SM_DT + SSM_H].set(p["a_log"][l])
    head_params = head_params.at[2, SM_DT:SM_DT + SSM_H].set(p["d_skip"][l])
    router_pad = jnp.zeros((D_MODEL, 128 - N_EXPERTS), F32)
    return dict(
        w_q=c(jnp.concatenate([idx_q * IDX_DIM ** -0.5, dsa_q, moba_q, fox_q], axis=1)),
        w_dsa_row=c(jnp.concatenate([dsa_kv, idx_k], axis=1)),
        w_moba_row=c(_interleave_heads(moba_k, moba_v, MOBA_H)),
        w_fox_kv=c(_interleave_heads(fox_k, fox_v, FOX_H)),
        w_z=c(ssm_z), w_xbc=c(ssm_xbc),
        w_small=c(jnp.concatenate([idx_w * IDX_H ** -0.5, ssm_dt, fox_f, pad], axis=1)),
        small_bias=small_bias, head_params=head_params,
        conv_w=p["conv_w"][l], conv_b=p["conv_b"][l], norm_g=p["ssm_norm_g"][l],
        projs=[c(p["w_br_dsa"][l]), c(p["w_br_moba"][l]), c(p["w_br_ssm"][l]), c(p["w_br_fox"][l])],
        w_gate=c(p["w_gate"][l]), b_gate=p["b_gate"][l], w_o=c(p["w_o"][l]),
        ln1=(p["ln1_g"][l], p["ln1_b"][l]), ln2=(p["ln2_g"][l], p["ln2_b"][l]), ln3=(p["ln3_g"][l], p["ln3_b"][l]),
        w_mq=c(p["w_mq"][l]), w_mkv=c(p["w_mkv"][l]), w_mo=c(p["w_mo"][l]),
        w_router=c(jnp.concatenate([p["w_router"][l], router_pad], axis=1)),
        b_router=jnp.concatenate([p["b_router"][l], jnp.full((128 - N_EXPERTS,), NEG_INF, F32)]).reshape(1, 128),
        w1=p["w1"], b1=p["b1"], w2=p["w2"], b2=p["b2"], layer=l,
    )


def in_proj(xb, lw, q_dtype=None):
    return dict(
        q_all=mm(xb, lw["w_q"], q_dtype or MXU_DTYPE),
        dsa_row=mm(xb, lw["w_dsa_row"]),
        moba_row=mm(xb, lw["w_moba_row"]),
        fox_kv=mm(xb, lw["w_fox_kv"]),
        z=mm(xb, lw["w_z"]),
        xbc=mm(xb, lw["w_xbc"]),
        small=mm(xb, lw["w_small"]),
    )


def mixer_prompt(xb, lw, nb, t):
    pr = in_proj(xb, lw)
    post, fcum = small_post(pr["small"], lw["small_bias"], nb, t)
    o_dsa = dsa_prompt(pr["q_all"], post, pr["dsa_row"], nb, t, min(DSA_TOPK, t // 4))
    o_moba = moba_prompt(pr["q_all"], pr["moba_row"], nb, t, min(MOBA_TOPK, (t - 1) // MOBA_BLOCK))
    o_ssm, ssm_new = ssd_prompt(pr["xbc"], pr["z"], post, lw["conv_w"], lw["conv_b"], lw["head_params"],
                                lw["norm_g"], nb, t)
    fk_rows = fcum[:, SM_FF:SM_FF + FOX_H].reshape(nb, t, FOX_H).transpose(0, 2, 1).reshape(nb * FOX_H, 1, t)
    o_fox = fox_prompt(pr["q_all"], pr["fox_kv"], fcum, fk_rows, nb, t)
    logf = post[:, SM_FF:SM_FF + FOX_H].reshape(nb, t, FOX_H, 1)
    rows = (pr["dsa_row"].reshape(nb, t, DSA_ROW),
            pr["moba_row"].reshape(nb, t, MOBA_H, 2 * HD),
            jnp.concatenate([pr["fox_kv"].reshape(nb, t, FOX_H, 2 * HD), logf], axis=-1),
            ssm_new,
            pr["xbc"].reshape(nb, t, CONV_DIM)[:, t - (SSM_CONV - 1):])
    return (o_dsa, o_moba, o_ssm, o_fox), rows


MOE_TOK = 128


def _router_kernel(h_ref, w_ref, b_ref, idx_ref, gate_ref, cnt_ref, carry_ref):
    i = pl.program_id(0)

    @pl.when(i == 0)
    def _():
        carry_ref[...] = jnp.zeros_like(carry_ref)

    work = _dot(h_ref[...], w_ref[...]) + b_ref[...]
    tm = work.shape[0]
    lane = lax.broadcasted_iota(I32, work.shape, 1)
    hot = jnp.zeros(work.shape, F32)
    vals, idxs = [], []
    for _k in range(TOP_K):
        m = jnp.max(work, axis=-1, keepdims=True)
        idx = jnp.min(jnp.where(work == m, lane, 128), axis=-1, keepdims=True)
        vals.append(m)
        idxs.append(idx)
        hot = hot + jnp.where(lane == idx, 1.0, 0.0)
        work = jnp.where(lane == idx, NEG_INF, work)
    es = [jnp.exp(v - vals[0]) for v in vals]
    tot = es[0] + es[1] + es[2] + es[3]
    strict = lax.broadcasted_iota(I32, (tm, tm), 1) < lax.broadcasted_iota(I32, (tm, tm), 0)
    prefix = _dot(jnp.where(strict, 1.0, 0.0), hot) + carry_ref[...]
    out_i = jnp.zeros(work.shape, I32)
    out_g = jnp.zeros(work.shape, F32)
    for k in range(TOP_K):
        rank = jnp.sum(jnp.where(lane == idxs[k], prefix, 0.0), axis=-1, keepdims=True)
        out_i = jnp.where(lane == k, idxs[k], out_i)
        out_i = jnp.where(lane == TOP_K + k, rank.astype(I32), out_i)
        out_g = jnp.where(lane == k, es[k] / tot, out_g)
    idx_ref[...] = out_i
    gate_ref[...] = out_g
    carry_ref[...] = carry_ref[...] + jnp.sum(hot, axis=0, keepdims=True)
    cnt_ref[...] = carry_ref[...]


def router(hb, w_router, b_router):
    n, d = hb.shape
    tm = MOE_TOK
    return pl.pallas_call(
        _router_kernel,
        grid=(n // tm,),
        in_specs=[pl.BlockSpec((tm, d), lambda i: (i, 0)),
                  pl.BlockSpec((d, 128), lambda i: (0, 0)),
                  pl.BlockSpec((1, 128), lambda i: (0, 0))],
        out_specs=[pl.BlockSpec((tm, 128), lambda i: (i, 0)),
                   pl.BlockSpec((tm, 128), lambda i: (i, 0)),
                   pl.BlockSpec((1, 128), lambda i: (0, 0))],
        out_shape=[jax.ShapeDtypeStruct((n, 128), I32), jax.ShapeDtypeStruct((n, 128), F32),
                   jax.ShapeDtypeStruct((1, 128), F32)],
        scratch_shapes=[pltpu.VMEM((1, 128), F32)],
        compiler_params=_params(("arbitrary",)),
        name="router",
    )(hb, w_router, b_router)


def _row_copy(src, dst, sem):
    return pltpu.make_async_copy(src, dst, sem)


def _pack_pairs(x):
    half = x.shape[1] // 2
    hi = lax.bitcast_convert_type(_rounded(x[:, :half]), I32) & (-65536)
    lo = lax.shift_right_logical(lax.bitcast_convert_type(_rounded(x[:, half:]), I32), 16)
    return hi | lo


def _unpack_pairs(u):
    hi = lax.bitcast_convert_type(u & (-65536), F32)
    lo = lax.bitcast_convert_type(lax.shift_left(u, 16), F32)
    return jnp.concatenate([hi, lo], axis=1)


def _dispatch_kernel(dest_ref, x_ref, xs_in_ref, xs_ref, pk_ref, sem):
    del xs_in_ref
    n = x_ref.shape[0]
    pk_ref[...] = _pack_pairs(x_ref[...])

    def issue(r, carry):
        for k in range(TOP_K):
            d = dest_ref[0, r * TOP_K + k]
            _row_copy(pk_ref.at[pl.ds(r, 1), :], xs_ref.at[pl.ds(d, 1), :], sem).start()
        return carry

    lax.fori_loop(0, n, issue, 0)

    def drain(r, carry):
        for _k in range(TOP_K):
            _row_copy(pk_ref.at[pl.ds(0, 1), :], xs_ref.at[pl.ds(0, 1), :], sem).wait()
        return carry

    lax.fori_loop(0, n, drain, 0)


def dispatch(dest3, x, n_rows):
    n, d = x.shape
    xs0 = jnp.zeros((n_rows, d // 2), I32)
    return pl.pallas_call(
        _dispatch_kernel,
        grid=(n // MOE_TOK,),
        in_specs=[pl.BlockSpec((None, 1, MOE_TOK * TOP_K), lambda i: (i, 0, 0), memory_space=pltpu.SMEM),
                  pl.BlockSpec((MOE_TOK, d), lambda i: (i, 0)),
                  pl.BlockSpec(memory_space=pl.ANY)],
        out_specs=pl.BlockSpec(memory_space=pl.ANY),
        out_shape=jax.ShapeDtypeStruct((n_rows, d // 2), I32),
        scratch_shapes=[pltpu.VMEM((MOE_TOK, d // 2), I32), pltpu.SemaphoreType.DMA(())],
        input_output_aliases={2: 0},
        compiler_params=_params(("arbitrary",)),
        name="moe_dispatch",
    )(dest3, x, xs0)


def _gmm1_kernel(be_ref, bf_ref, xs_ref, wg_ref, wl_ref, bg_ref, bl_ref, o_ref, wgb_ref, wlb_ref):
    del be_ref
    i = pl.program_id(1)
    flags = bf_ref[i]

    @pl.when((flags & 2) != 0)
    def _():
        wgb_ref[...] = wg_ref[...].astype(wgb_ref.dtype)
        wlb_ref[...] = wl_ref[...].astype(wlb_ref.dtype)

    @pl.when((flags & 1) != 0)
    def _():
        x = _unpack_pairs(xs_ref[...]).astype(MXU_DTYPE)
        g = jnp.minimum(_dot(x, wgb_ref[...]) + bg_ref[...], SWIGLU_LIMIT)
        lin = jnp.clip(_dot(x, wlb_ref[...]) + bl_ref[...], -SWIGLU_LIMIT, SWIGLU_LIMIT)
        o_ref[...] = (g * _sigmoid(SWIGLU_ALPHA * g) * (lin + 1.0)).astype(o_ref.dtype)

    @pl.when((flags & 1) == 0)
    def _():
        o_ref[...] = jnp.zeros_like(o_ref)


def gmm1(blk_e, blk_flags, xs, w1, b1, layer):
    s, d_packed = xs.shape
    d = w1.shape[2]
    nblk = s // MOE_ROWS
    tn = 512
    nj = D_FF // tn
    gs = pltpu.PrefetchScalarGridSpec(
        num_scalar_prefetch=2, grid=(nj, nblk),
        in_specs=[pl.BlockSpec((MOE_ROWS, d_packed), lambda j, i, be, bf: (i, 0)),
                  pl.BlockSpec((None, None, d, tn), lambda j, i, be, bf: (layer, be[i], 0, j)),
                  pl.BlockSpec((None, None, d, tn), lambda j, i, be, bf: (layer, be[i], 0, nj + j)),
                  pl.BlockSpec((None, None, 1, tn), lambda j, i, be, bf: (layer, be[i], 0, j)),
                  pl.BlockSpec((None, None, 1, tn), lambda j, i, be, bf: (layer, be[i], 0, nj + j))],
        out_specs=pl.BlockSpec((MOE_ROWS, tn), lambda j, i, be, bf: (i, j)),
        scratch_shapes=[pltpu.VMEM((d, tn), MXU_DTYPE), pltpu.VMEM((d, tn), MXU_DTYPE)])
    b1r = b1.reshape(DEPTH, N_EXPERTS, 1, 2 * D_FF)
    return pl.pallas_call(
        _gmm1_kernel, grid_spec=gs,
        out_shape=jax.ShapeDtypeStruct((s, D_FF), MXU_DTYPE),
        compiler_params=_params(("arbitrary", "arbitrary")),
        name="moe_gmm1",
    )(blk_e, blk_flags, xs, w1, w1, b1r, b1r)


def _gmm2_kernel(be_ref, bf_ref, h_ref, w_ref, b_ref, o_ref, wb_ref):
    del be_ref
    i = pl.program_id(1)
    flags = bf_ref[i]

    @pl.when((flags & 2) != 0)
    def _():
        wb_ref[...] = w_ref[...].astype(wb_ref.dtype)

    @pl.when((flags & 1) != 0)
    def _():
        o_ref[...] = _dot(h_ref[...], wb_ref[...]) + b_ref[...]

    @pl.when((flags & 1) == 0)
    def _():
        o_ref[...] = jnp.zeros_like(o_ref)


def gmm2(blk_e, blk_flags, hs, w2, b2, layer):
    s, f = hs.shape
    d = w2.shape[3]
    nblk = s // MOE_ROWS
    tn = 1024
    nj = d // tn
    gs = pltpu.PrefetchScalarGridSpec(
        num_scalar_prefetch=2, grid=(nj, nblk),
        in_specs=[pl.BlockSpec((MOE_ROWS, f), lambda j, i, be, bf: (i, 0)),
                  pl.BlockSpec((None, None, f, tn), lambda j, i, be, bf: (layer, be[i], 0, j)),
                  pl.BlockSpec((None, None, 1, tn), lambda j, i, be, bf: (layer, be[i], 0, j))],
        out_specs=pl.BlockSpec((MOE_ROWS, tn), lambda j, i, be, bf: (i, j)),
        scratch_shapes=[pltpu.VMEM((f, tn), MXU_DTYPE)])
    return pl.pallas_call(
        _gmm2_kernel, grid_spec=gs,
        out_shape=jax.ShapeDtypeStruct((s, d), F32),
        compiler_params=_params(("arbitrary", "arbitrary")),
        name="moe_gmm2",
    )(blk_e, blk_flags, hs, w2, b2.reshape(DEPTH, N_EXPERTS, 1, d))


def _combine_kernel(dcur_ref, dnxt_ref, gate_ref, h_ref, g_ref, b_ref, ys_ref, o_ref, ob_ref, buf_ref, sem):
    i = pl.program_id(0)
    n_steps = pl.num_programs(0)
    n = h_ref.shape[0]
    slot = lax.rem(i, 2)

    def issue(d_ref, s):
        def body(r, carry):
            for k in range(TOP_K):
                d = d_ref[0, r * TOP_K + k]
                _row_copy(ys_ref.at[pl.ds(d, 1), :], buf_ref.at[s, k, pl.ds(r, 1), :], sem.at[s]).start()
            return carry
        lax.fori_loop(0, n, body, 0)

    @pl.when(i == 0)
    def _():
        issue(dcur_ref, 0)

    @pl.when(i + 1 < n_steps)
    def _():
        issue(dnxt_ref, 1 - slot)

    def drain(r, carry):
        for k in range(TOP_K):
            _row_copy(ys_ref.at[pl.ds(0, 1), :], buf_ref.at[slot, k, pl.ds(0, 1), :], sem.at[slot]).wait()
        return carry

    lax.fori_loop(0, n, drain, 0)
    gate = gate_ref[...]
    moe = gate[:, 0:1] * buf_ref[slot, 0]
    for k in range(1, TOP_K):
        moe = moe + gate[:, k:k + 1] * buf_ref[slot, k]
    y = _layer_norm(ALPHA * h_ref[...] + moe, g_ref[...], b_ref[...])
    o_ref[...] = y
    ob_ref[...] = y.astype(ob_ref.dtype)


def combine_ln(dest3, gate, h, ys, g, b):
    n, d = h.shape
    nt = n // MOE_TOK
    smem = lambda f: pl.BlockSpec((None, 1, MOE_TOK * TOP_K), f, memory_space=pltpu.SMEM)
    return pl.pallas_call(
        _combine_kernel,
        grid=(nt,),
        in_specs=[smem(lambda i: (i, 0, 0)),
                  smem(lambda i: (jnp.minimum(i + 1, nt - 1), 0, 0)),
                  pl.BlockSpec((MOE_TOK, 128), lambda i: (i, 0)),
                  pl.BlockSpec((MOE_TOK, d), lambda i: (i, 0)),
                  pl.BlockSpec((1, d), lambda i: (0, 0)),
                  pl.BlockSpec((1, d), lambda i: (0, 0)),
                  pl.BlockSpec(memory_space=pl.ANY)],
        out_specs=[pl.BlockSpec((MOE_TOK, d), lambda i: (i, 0)),
                   pl.BlockSpec((MOE_TOK, d), lambda i: (i, 0))],
        out_shape=[jax.ShapeDtypeStruct((n, d), F32), jax.ShapeDtypeStruct((n, d), MXU_DTYPE)],
        scratch_shapes=[pltpu.VMEM((2, TOP_K, MOE_TOK, d), F32), pltpu.SemaphoreType.DMA((2,))],
        compiler_params=_params(("arbitrary",)),
        name="moe_combine",
    )(dest3, dest3, gate, h, g.reshape(1, d), b.reshape(1, d), ys)


def moe_ln(h, hb, lw):
    n, d = h.shape
    idx, gate, cnt = router(hb, lw["w_router"], lw["b_router"])
    counts = cnt[0, :N_EXPERTS].astype(I32)
    padded = (counts + MOE_ROWS - 1) // MOE_ROWS * MOE_ROWS
    pad_end = jnp.cumsum(padded)
    pad_start = pad_end - padded
    dest = pad_start[idx[:, :TOP_K]] + idx[:, TOP_K:2 * TOP_K]
    dest3 = dest.reshape(n // MOE_TOK, 1, MOE_TOK * TOP_K)
    nblk = -(-(n * TOP_K) // MOE_ROWS) + N_EXPERTS
    blk_start = jnp.arange(nblk, dtype=I32) * MOE_ROWS
    blk_e = jnp.minimum(jnp.sum((blk_start[:, None] >= pad_end[None, :]).astype(I32), axis=1), N_EXPERTS - 1)
    valid = (blk_start < pad_end[-1]).astype(I32)
    first = jnp.concatenate([jnp.ones((1,), I32), (blk_e[1:] != blk_e[:-1]).astype(I32)])
    blk_flags = valid + 2 * first
    xs = dispatch(dest3, h, nblk * MOE_ROWS)
    hs = gmm1(blk_e, blk_flags, xs, lw["w1"], lw["b1"], lw["layer"])
    ys = gmm2(blk_e, blk_flags, hs, lw["w2"], lw["b2"], lw["layer"])
    return combine_ln(dest3, gate, h, ys, *lw["ln3"])


def _rows8(x):
    r = x.shape[0]
    return x if r == 8 else jnp.concatenate([x, jnp.zeros((8 - r, x.shape[1]), x.dtype)], axis=0)


def _paged_attend(lgs, lg_new, vs, v_new, exact=False, v_t=False):
    r = lg_new.shape[0]
    rnd = (lambda x: x) if exact else _rounded
    if v_t:
        pv = _dot_nt_exact if exact else _dot_nt
    else:
        pv = _dot_exact if exact else _dot
    m = lg_new
    for lg in lgs:
        m = jnp.maximum(m, jnp.max(lg, axis=-1, keepdims=True))
    p_new = jnp.exp(lg_new - m)
    ps = [jnp.exp(lg - m) for lg in lgs]
    tot = p_new
    for p in ps:
        tot = tot + jnp.sum(p, axis=-1, keepdims=True)
    acc = rnd(p_new / tot) * rnd(v_new)
    for p, v in zip(ps, vs):
        acc = acc + pv(_rows8(p / tot), v)[0:r, :]
    return acc


def _page_specs(shape_tail, layer, n_pages):
    nd = len(shape_tail)
    return [pl.BlockSpec((None, None) + shape_tail,
                         functools.partial(lambda b, pt, pg: (layer, pt[b, pg]) + (0,) * nd, pg=pg))
            for pg in range(n_pages)]


def _dsa_dec_score_kernel(pt_ref, *refs, n_pages):
    del pt_ref
    pages = refs[:n_pages]
    iq_ref, iw_ref, new_ref, o_ref = refs[n_pages:]
    iq = iq_ref[...]
    iw = _rounded(iw_ref[...])
    for pg in range(n_pages):
        s = _dot(iq, pages[pg][2 * HD:2 * HD + IDX_DIM, :])
        o_ref[pg:pg + 1, :] = jnp.sum(_rounded(jnp.maximum(s, 0.0)) * iw, axis=0, keepdims=True)
    s_new = jnp.sum(iq.astype(F32) * _rounded(new_ref[:, 2 * HD:2 * HD + IDX_DIM]), axis=-1, keepdims=True)
    sc_new = jnp.sum(_rounded(jnp.maximum(s_new, 0.0)) * iw, axis=0, keepdims=True)
    o_ref[n_pages:n_pages + 1, :] = jnp.broadcast_to(sc_new, (1, PAGE))


def dsa_dec_scores(page_table, cache_dsa, layer, iq, iw, new_rows):
    nb, n_pages = page_table.shape
    gs = pltpu.PrefetchScalarGridSpec(
        num_scalar_prefetch=1, grid=(nb,),
        in_specs=_page_specs((DSA_ROW, PAGE), layer, n_pages) + [
            pl.BlockSpec((None, IDX_H, IDX_DIM), lambda b, pt: (b, 0, 0)),
            pl.BlockSpec((None, IDX_H, 1), lambda b, pt: (b, 0, 0)),
            pl.BlockSpec((None, 1, DSA_ROW), lambda b, pt: (b, 0, 0))],
        out_specs=pl.BlockSpec((None, n_pages + 1, PAGE), lambda b, pt: (b, 0, 0)))
    return pl.pallas_call(
        functools.partial(_dsa_dec_score_kernel, n_pages=n_pages), grid_spec=gs,
        out_shape=jax.ShapeDtypeStruct((nb, n_pages + 1, PAGE), F32),
        compiler_params=_params(("parallel",)),
        name="dsa_dec_scores",
    )(page_table, *([cache_dsa] * n_pages), iq, iw, new_rows)


def _topk_mask_kernel(s_ref, o_ref, *, n_keys, n_top):
    s = s_ref[...]
    valid = lax.broadcasted_iota(I32, s.shape, 1) < n_keys
    key = jnp.where(valid, _sort_key(s), INT_MIN)
    thr = _kth_largest_key(lambda c: jnp.sum(jnp.where(key >= c, 1.0, 0.0), axis=-1, keepdims=True),
                           n_top, (s.shape[0], 1))
    o_ref[...] = jnp.where(valid, jnp.where(key >= thr, 1.0, 0.0), 0.0)


def topk_mask(scores, n_keys, n_top):
    return pl.pallas_call(
        functools.partial(_topk_mask_kernel, n_keys=n_keys, n_top=n_top),
        out_shape=jax.ShapeDtypeStruct(scores.shape, F32),
        compiler_params=pltpu.CompilerParams(vmem_limit_bytes=V7X_VMEM_LIMIT),
        name="topk_mask",
    )(scores)


def _dsa_dec_attend_kernel(pt_ref, *refs, n_pages):
    del pt_ref
    pages = refs[:n_pages]
    q_ref, mask_ref, new_ref, o_ref = refs[n_pages:]
    q = q_ref[...]
    lgs, vs = [], []
    for pg in range(n_pages):
        lg = _dot(_rows8(q), pages[pg][0:HD, :])[0:DSA_H, :] * HD ** -0.5
        lgs.append(jnp.where(mask_ref[pg:pg + 1, :] > 0.5, lg, NEG_INF))
        vs.append(pages[pg][HD:2 * HD, :])
    lg_new = jnp.sum(q.astype(F32) * _rounded(new_ref[:, 0:HD]), axis=-1, keepdims=True) * HD ** -0.5
    lg_new = jnp.where(mask_ref[n_pages:n_pages + 1, 0:1] > 0.5, lg_new, NEG_INF)
    o_ref[...] = _paged_attend(lgs, lg_new, vs, new_ref[:, HD:2 * HD], v_t=True).astype(o_ref.dtype)


def dsa_dec_attend(page_table, cache_dsa, layer, q, mask, new_rows):
    nb, n_pages = page_table.shape
    gs = pltpu.PrefetchScalarGridSpec(
        num_scalar_prefetch=1, grid=(nb,),
        in_specs=_page_specs((DSA_ROW, PAGE), layer, n_pages) + [
            pl.BlockSpec((None, DSA_H, HD), lambda b, pt: (b, 0, 0)),
            pl.BlockSpec((None, n_pages + 1, PAGE), lambda b, pt: (b, 0, 0)),
            pl.BlockSpec((None, 1, DSA_ROW), lambda b, pt: (b, 0, 0))],
        out_specs=pl.BlockSpec((None, DSA_H, HD), lambda b, pt: (b, 0, 0)))
    return pl.pallas_call(
        functools.partial(_dsa_dec_attend_kernel, n_pages=n_pages), grid_spec=gs,
        out_shape=jax.ShapeDtypeStruct((nb, DSA_H, HD), MXU_DTYPE),
        compiler_params=_params(("parallel",)),
        name="dsa_dec_attend",
    )(page_table, *([cache_dsa] * n_pages), q, mask, new_rows)


def _page_head_copy(cache_ref, layer, page, h, buf_ref, slot, pg, sem_ref):
    return pltpu.make_async_copy(cache_ref.at[layer, page, :, h, :], buf_ref.at[slot, pg, h], sem_ref.at[slot])


def _double_buffered_pages(pt_ref, cache_ref, layer, buf_ref, sem_ref, n_pages, n_heads):
    b = pl.program_id(0)
    nb = pl.num_programs(0)
    slot = lax.rem(b, 2)

    def fetch(seq, s):
        for pg in range(n_pages):
            page = pt_ref[seq, pg]
            for h in range(n_heads):
                _page_head_copy(cache_ref, layer, page, h, buf_ref, s, pg, sem_ref).start()

    @pl.when(b == 0)
    def _():
        fetch(b, 0)

    @pl.when(b + 1 < nb)
    def _():
        fetch(b + 1, 1 - slot)

    for pg in range(n_pages):
        for h in range(n_heads):
            _page_head_copy(cache_ref, layer, 0, h, buf_ref, slot, pg, sem_ref).wait()
    return slot


def _moba_dec_kernel(pt_ref, cache_ref, q_ref, new_ref, o_ref, buf_ref, sem_ref, *, layer, n_pages, n_sel):
    slot = _double_buffered_pages(pt_ref, cache_ref, layer, buf_ref, sem_ref, n_pages, MOBA_H)
    ppb = MOBA_BLOCK // PAGE
    nblk = n_pages // ppb
    for h in range(MOBA_H):
        q = q_ref[h:h + 1, :]
        ks = [buf_ref[slot, pg, h, :, 0:HD] for pg in range(n_pages)]
        lane = lax.broadcasted_iota(I32, (1, 128), 1)
        gsc = jnp.full((1, 128), NEG_INF, F32)
        for n in range(nblk):
            tot = jnp.sum(ks[n * ppb], axis=0, keepdims=True)
            for j in range(1, ppb):
                tot = tot + jnp.sum(ks[n * ppb + j], axis=0, keepdims=True)
            mean = tot * (1.0 / MOBA_BLOCK)
            gsc = jnp.where(lane == n, jnp.sum(q * mean, axis=-1, keepdims=True), gsc)
        selm = _top_rank_mask(gsc[:, 0:nblk], n_sel)
        k_new = new_ref[:, h * 2 * HD:h * 2 * HD + HD]
        v_new = new_ref[:, h * 2 * HD + HD:(h + 1) * 2 * HD]
        lg_new = jnp.sum(q * k_new, axis=-1, keepdims=True) * HD ** -0.5
        lgs = []
        m = lg_new
        for pg in range(n_pages):
            lg = jnp.sum(ks[pg] * q, axis=-1, keepdims=True) * HD ** -0.5
            lg = jnp.where(selm[:, pg // ppb:pg // ppb + 1] > 0.5, lg, NEG_INF)
            lgs.append(lg)
            m = jnp.maximum(m, jnp.max(lg, axis=0, keepdims=True))
        p_new = jnp.exp(lg_new - m)
        ps = [jnp.exp(lg - m) for lg in lgs]
        tot = p_new
        for pcol in ps:
            tot = tot + jnp.sum(pcol, axis=0, keepdims=True)
        acc = jnp.zeros((PAGE, HD), F32)
        for pg in range(n_pages):
            acc = acc + _rounded(buf_ref[slot, pg, h, :, HD:2 * HD]) * _rounded(ps[pg] / tot)
        out = jnp.sum(acc, axis=0, keepdims=True) + _rounded(p_new / tot) * _rounded(v_new)
        o_ref[h:h + 1, :] = out.astype(o_ref.dtype)


def moba_dec(page_table, cache_moba, layer, q, new_rows):
    nb, n_pages = page_table.shape
    n_keys = n_pages * PAGE + 1
    n_sel = min(MOBA_TOPK, (n_keys - 1) // MOBA_BLOCK)
    gs = pltpu.PrefetchScalarGridSpec(
        num_scalar_prefetch=1, grid=(nb,),
        in_specs=[pl.BlockSpec(memory_space=pl.ANY),
                  pl.BlockSpec((None, MOBA_H, HD), lambda b, pt: (b, 0, 0)),
                  pl.BlockSpec((None, 1, MOBA_H * 2 * HD), lambda b, pt: (b, 0, 0))],
        out_specs=pl.BlockSpec((None, MOBA_H, HD), lambda b, pt: (b, 0, 0)),
        scratch_shapes=[pltpu.VMEM((2, n_pages, MOBA_H, PAGE, 2 * HD), F32), pltpu.SemaphoreType.DMA((2,))])
    return pl.pallas_call(
        functools.partial(_moba_dec_kernel, layer=layer, n_pages=n_pages, n_sel=n_sel), grid_spec=gs,
        out_shape=jax.ShapeDtypeStruct((nb, MOBA_H, HD), MXU_DTYPE),
        compiler_params=_params(("arbitrary",)),
        name="moba_dec",
    )(page_table, cache_moba, q, new_rows)


def _fox_dec_kernel(pt_ref, cache_ref, q_ref, qt_ref, new_ref, post_ref, ot_ref, buf_ref, sem_ref, *,
                    layer, n_pages):
    slot = _double_buffered_pages(pt_ref, cache_ref, layer, buf_ref, sem_ref, n_pages, FOX_H)
    after = jnp.where(lax.broadcasted_iota(I32, (PAGE, PAGE), 0) > lax.broadcasted_iota(I32, (PAGE, PAGE), 1),
                      1.0, 0.0)
    diag = lax.broadcasted_iota(I32, (HD, HD), 0) == lax.broadcasted_iota(I32, (HD, HD), 1)
    lf_all = jnp.concatenate([buf_ref[slot, pg, h, 2 * HD:2 * HD + 1, :]
                              for pg in range(n_pages) for h in range(FOX_H)], axis=0)
    within_all = _dot_exact(lf_all, after)
    tot_all = jnp.sum(lf_all, axis=-1, keepdims=True)
    for h in range(FOX_H):
        q_col = qt_ref[:, h:h + 1]
        later = post_ref[:, SM_FF + h:SM_FF + h + 1]
        k_new = new_ref[:, h * 2 * HD:h * 2 * HD + HD]
        v_new = new_ref[:, h * 2 * HD + HD:(h + 1) * 2 * HD]
        lg_new = jnp.sum(q_ref[h:h + 1, :] * k_new, axis=-1, keepdims=True) * HD ** -0.5
        lgs = [None] * n_pages
        m = lg_new
        for pg in range(n_pages - 1, -1, -1):
            i = pg * FOX_H + h
            lg = jnp.sum(buf_ref[slot, pg, h, 0:HD, :] * q_col, axis=0, keepdims=True) * HD ** -0.5
            lgs[pg] = lg + (within_all[i:i + 1, :] + later)
            later = later + tot_all[i:i + 1, :]
            m = jnp.maximum(m, jnp.max(lgs[pg], axis=-1, keepdims=True))
        p_new = jnp.exp(lg_new - m)
        ps = [jnp.exp(lg - m) for lg in lgs]
        tot = p_new
        for prow in ps:
            tot = tot + jnp.sum(prow, axis=-1, keepdims=True)
        acc = jnp.zeros((HD, PAGE), F32)
        for pg in range(n_pages):
            acc = acc + buf_ref[slot, pg, h, HD:2 * HD, :] * (ps[pg] / tot)
        v_new_col = jnp.sum(jnp.where(diag, v_new, 0.0), axis=1, keepdims=True)
        ot_ref[:, h:h + 1] = jnp.sum(acc, axis=-1, keepdims=True) + v_new_col * (p_new / tot)


def fox_dec(page_table, cache_fox_t, layer, q, new_kv, post):
    nb, n_pages = page_table.shape
    gs = pltpu.PrefetchScalarGridSpec(
        num_scalar_prefetch=1, grid=(nb,),
        in_specs=[pl.BlockSpec(memory_space=pl.ANY),
                  pl.BlockSpec((None, FOX_H, HD), lambda b, pt: (b, 0, 0)),
                  pl.BlockSpec((None, HD, FOX_H), lambda b, pt: (b, 0, 0)),
                  pl.BlockSpec((None, 1, FOX_H * 2 * HD), lambda b, pt: (b, 0, 0)),
                  pl.BlockSpec((None, 1, 128), lambda b, pt: (b, 0, 0))],
        out_specs=pl.BlockSpec((None, HD, FOX_H), lambda b, pt: (b, 0, 0)),
        scratch_shapes=[pltpu.VMEM((2, n_pages, FOX_H, 2 * HD + 1, PAGE), F32), pltpu.SemaphoreType.DMA((2,))])
    out_t = pl.pallas_call(
        functools.partial(_fox_dec_kernel, layer=layer, n_pages=n_pages), grid_spec=gs,
        out_shape=jax.ShapeDtypeStruct((nb, HD, FOX_H), F32),
        compiler_params=_params(("arbitrary",)),
        name="fox_dec",
    )(page_table, cache_fox_t, q, jnp.transpose(q, (0, 2, 1)), new_kv, post)
    return jnp.transpose(out_t, (0, 2, 1)).astype(MXU_DTYPE)


def _ssd_dec_kernel(st_ref, cs_ref, xbc_ref, z_ref, post_ref, cw_ref, cb_ref, hp_ref, ng_ref,
                    o_ref, sto_ref, cso_ref):
    cs = cs_ref[...]
    xn = xbc_ref[...]
    cw = cw_ref[...]
    conv = cw[0:1, :] * cs[0:1, :]
    for j in range(1, SSM_CONV - 1):
        conv = conv + cw[j:j + 1, :] * cs[j:j + 1, :]
    conv = conv + cw[SSM_CONV - 1:SSM_CONV, :] * xn + cb_ref[...]
    cso_ref[0:SSM_CONV - 2, :] = cs[1:SSM_CONV - 1, :]
    cso_ref[SSM_CONV - 2:SSM_CONV - 1, :] = xn
    u = _silu(conv)
    post = post_ref[...]
    a_row = -jnp.exp(hp_ref[1:2, :])
    diag = lax.broadcasted_iota(I32, (SSM_P, SSM_P), 0) == lax.broadcasted_iota(I32, (SSM_P, SSM_P), 1)
    ys = []
    for h in range(SSM_H):
        g = h // (SSM_H // SSM_G)
        bg = u[:, SSM_DI + g * SSM_N:SSM_DI + (g + 1) * SSM_N]
        cg = u[:, SSM_DI + SSM_G * SSM_N + g * SSM_N:SSM_DI + SSM_G * SSM_N + (g + 1) * SSM_N]
        dt = post[:, SM_DT + h:SM_DT + h + 1]
        adt = dt * a_row[:, SM_DT + h:SM_DT + h + 1]
        x_h = u[:, h * SSM_P:(h + 1) * SSM_P]
        xdt = x_h * dt
        y_diag = xdt * jnp.sum(cg * bg, axis=-1, keepdims=True)
        xdt_col = jnp.sum(jnp.where(diag, xdt, 0.0), axis=1, keepdims=True)
        h0 = _rounded(st_ref[h])
        sto_ref[h] = _rounded(jnp.exp(adt)) * h0 + _rounded(xdt_col * bg)
        y_col = jnp.sum(h0 * _rounded(cg), axis=1, keepdims=True)
        y_off = jnp.exp(adt) * jnp.sum(jnp.where(diag, y_col, 0.0), axis=0, keepdims=True)
        ys.append(y_diag + y_off + x_h * hp_ref[2:3, SM_DT + h:SM_DT + h + 1])
    y = jnp.concatenate(ys, axis=1)
    o_ref[...] = _gated_group_norm(y, z_ref[...], ng_ref[...]).astype(o_ref.dtype)


def ssd_dec(state_ssm, state_conv, layer, xbc, z, post, conv_w, conv_b, head_params, norm_g):
    nb = xbc.shape[0]
    row = lambda w: pl.BlockSpec((None, 1, w), lambda b: (b, 0, 0))
    const = lambda r, w: pl.BlockSpec((r, w), lambda b: (0, 0))
    return pl.pallas_call(
        _ssd_dec_kernel,
        grid=(nb,),
        in_specs=[pl.BlockSpec((None, None, SSM_H, SSM_P, SSM_N), lambda b: (layer, b, 0, 0, 0)),
                  pl.BlockSpec((None, None, SSM_CONV - 1, CONV_DIM), lambda b: (layer, b, 0, 0)),
                  row(CONV_DIM), row(SSM_DI), row(128),
                  const(SSM_CONV, CONV_DIM), const(1, CONV_DIM), const(8, 128), const(1, SSM_DI)],
        out_specs=[row(SSM_DI),
                   pl.BlockSpec((None, SSM_H, SSM_P, SSM_N), lambda b: (b, 0, 0, 0)),
                   pl.BlockSpec((None, SSM_CONV - 1, CONV_DIM), lambda b: (b, 0, 0))],
        out_shape=[jax.ShapeDtypeStruct((nb, 1, SSM_DI), MXU_DTYPE),
                   jax.ShapeDtypeStruct((nb, SSM_H, SSM_P, SSM_N), F32),
                   jax.ShapeDtypeStruct((nb, SSM_CONV - 1, CONV_DIM), F32)],
        compiler_params=_params(("parallel",)),
        name="ssd_dec",
    )(state_ssm, state_conv, xbc.reshape(nb, 1, CONV_DIM), z.reshape(nb, 1, SSM_DI), post.reshape(nb, 1, 128),
      conv_w, conv_b.reshape(1, CONV_DIM), head_params, norm_g.reshape(1, SSM_DI))


def _mem_dec_kernel(q_ref, kv_ref, o_ref):
    for h in range(MEM_H):
        q = q_ref[h:h + 1, :]
        lg = _dot_nt(_rows8(q), kv_ref[:, h, 0:HD])[0:1, :] * HD ** -0.5
        m = jnp.max(lg, axis=-1, keepdims=True)
        p = jnp.exp(lg - m)
        o = _dot(_rows8(p / jnp.sum(p, axis=-1, keepdims=True)), kv_ref[:, h, HD:2 * HD])[0:1, :]
        o_ref[h:h + 1, :] = o.astype(o_ref.dtype)


def mem_dec(mq, cache_mem, layer):
    nb = mq.shape[0]
    mem_len = cache_mem.shape[2]
    return pl.pallas_call(
        _mem_dec_kernel,
        grid=(nb,),
        in_specs=[pl.BlockSpec((None, MEM_H, HD), lambda b: (b, 0, 0)),
                  pl.BlockSpec((None, None, mem_len, MEM_H, 2 * HD), lambda b: (layer, b, 0, 0, 0))],
        out_specs=pl.BlockSpec((None, MEM_H, HD), lambda b: (b, 0, 0)),
        out_shape=jax.ShapeDtypeStruct((nb, MEM_H, HD), MXU_DTYPE),
        compiler_params=_params(("parallel",)),
        name="mem_dec",
    )(mq, cache_mem)


def mixer_sample(xb, lw, layer, p):
    nb = xb.shape[0]
    page_table = p["page_table"]
    n_pages = page_table.shape[1]
    n_keys = n_pages * PAGE + 1
    cache_dsa_t = jnp.transpose(p["cache_dsa"], (0, 1, 3, 2))
    cache_fox_t = jnp.transpose(p["cache_fox"], (0, 1, 4, 3, 2))
    pr = in_proj(xb, lw, F32)
    post, _ = small_post(pr["small"], lw["small_bias"], 1, nb)
    q_all = pr["q_all"]
    q_mxu = q_all[:, :Q_MOBA].astype(MXU_DTYPE)
    dsa_new = pr["dsa_row"].reshape(nb, 1, DSA_ROW)
    scores = dsa_dec_scores(page_table, cache_dsa_t, layer,
                            q_mxu[:, Q_IDX:Q_IDX + IDX_H * IDX_DIM].reshape(nb, IDX_H, IDX_DIM),
                            post[:, SM_IW:SM_IW + IDX_H].reshape(nb, IDX_H, 1), dsa_new)
    mask = topk_mask(scores.reshape(nb, (n_pages + 1) * PAGE), n_keys, min(DSA_TOPK, n_keys // 4))
    o_dsa = dsa_dec_attend(page_table, cache_dsa_t, layer, q_mxu[:, Q_DSA:Q_DSA + DSA_H * HD].reshape(nb, DSA_H, HD),
                           mask.reshape(nb, n_pages + 1, PAGE), dsa_new)
    o_moba = moba_dec(page_table, p["cache_moba"], layer, q_all[:, Q_MOBA:Q_MOBA + MOBA_H * HD].reshape(nb, MOBA_H, HD),
                      pr["moba_row"].reshape(nb, 1, MOBA_H * 2 * HD))
    o_fox = fox_dec(page_table, cache_fox_t, layer, q_all[:, Q_FOX:Q_FOX + FOX_H * HD].reshape(nb, FOX_H, HD),
                    pr["fox_kv"].reshape(nb, 1, FOX_H * 2 * HD), post.reshape(nb, 1, 128))
    o_ssm, ssm_new, conv_new = ssd_dec(p["state_ssm"], p["state_conv"], layer, pr["xbc"], pr["z"], post,
                                       lw["conv_w"], lw["conv_b"], lw["head_params"], lw["norm_g"])
    logf = post[:, SM_FF:SM_FF + FOX_H].reshape(nb, 1, FOX_H, 1)
    rows = (dsa_new,
            pr["moba_row"].reshape(nb, 1, MOBA_H, 2 * HD),
            jnp.concatenate([pr["fox_kv"].reshape(nb, 1, FOX_H, 2 * HD), logf], axis=-1),
            ssm_new, conv_new)
    outs = (o_dsa.reshape(nb, DSA_H * HD), o_moba.reshape(nb, MOBA_H * HD), o_ssm.reshape(nb, SSM_DI),
            o_fox.reshape(nb, FOX_H * HD))
    return outs, rows


def kernel(x_prompt, x_sample, mem_prompt, cache_dsa, cache_moba, cache_fox, cache_mem, state_ssm, state_conv,
           page_table, w_in, fox_b_f, conv_w, conv_b, dt_bias, a_log, d_skip, ssm_norm_g, w_br_dsa, w_br_moba,
           w_br_ssm, w_br_fox, w_gate, b_gate, w_o, ln1_g, ln1_b, w_mq, w_mkv, w_mo, ln2_g, ln2_b, w_router,
           b_router, w1, b1, w2, b2, ln3_g, ln3_b):
    p = dict(cache_dsa=cache_dsa, cache_moba=cache_moba, cache_fox=cache_fox, cache_mem=cache_mem,
             state_ssm=state_ssm, state_conv=state_conv, page_table=page_table, w_in=w_in, fox_b_f=fox_b_f,
             conv_w=conv_w, conv_b=conv_b, dt_bias=dt_bias, a_log=a_log, d_skip=d_skip, ssm_norm_g=ssm_norm_g,
             w_br_dsa=w_br_dsa, w_br_moba=w_br_moba, w_br_ssm=w_br_ssm, w_br_fox=w_br_fox, w_gate=w_gate,
             b_gate=b_gate, w_o=w_o, ln1_g=ln1_g, ln1_b=ln1_b, w_mq=w_mq, w_mkv=w_mkv, w_mo=w_mo, ln2_g=ln2_g,
             ln2_b=ln2_b, w_router=w_router, b_router=b_router, w1=w1, b1=b1, w2=w2, b2=b2, ln3_g=ln3_g,
             ln3_b=ln3_b)
    nb, t, d = x_prompt.shape
    ns = x_sample.shape[0]
    mem_len = mem_prompt.shape[1]
    n_p = nb * t
    xp, xs = x_prompt.reshape(n_p, d), x_sample.reshape(ns, d)
    xpb, xsb = xp.astype(MXU_DTYPE), xs.astype(MXU_DTYPE)
    memb = mem_prompt.reshape(nb * mem_len, d).astype(MXU_DTYPE)
    st_p, st_s = [], []
    for l in range(DEPTH):
        lw = prep_layer(l, p)
        outs_p, rows_p = mixer_prompt(xpb, lw, nb, t)
        outs_s, rows_s = mixer_sample(xsb, lw, l, p)
        h1p, h1pb = mm_ln(gate_merge(xpb, outs_p, lw["w_gate"], lw["b_gate"], lw["projs"]), lw["w_o"], xp, *lw["ln1"])
        h1s, h1sb = mm_ln(gate_merge(xsb, outs_s, lw["w_gate"], lw["b_gate"], lw["projs"]), lw["w_o"], xs, *lw["ln1"])
        mem_kv = mm(memb, lw["w_mkv"])
        om_p = mem_prompt_attn(mm(h1pb, lw["w_mq"], MXU_DTYPE), mem_kv, nb, t, mem_len)
        om_s = mem_dec(mm(h1sb, lw["w_mq"], MXU_DTYPE).reshape(ns, MEM_H, HD), cache_mem, l).reshape(ns, MEM_H * HD)
        h2p, h2pb = mm_ln(om_p, lw["w_mo"], h1p, *lw["ln2"])
        h2s, h2sb = mm_ln(om_s, lw["w_mo"], h1s, *lw["ln2"])
        y, yb = moe_ln(jnp.concatenate([h2p, h2s], axis=0), jnp.concatenate([h2pb, h2sb], axis=0), lw)
        xp, xs, xpb, xsb = y[:n_p], y[n_p:], yb[:n_p], yb[n_p:]
        st_p.append(rows_p + (mem_kv.reshape(nb, mem_len, MEM_H, 2 * HD),))
        st_s.append(rows_s)
    sp = [jnp.stack(z) for z in zip(*st_p)]
    ss = [jnp.stack(z) for z in zip(*st_s)]
    return (xp.reshape(nb, t, d), xs.reshape(ns, 1, d), sp[0], ss[0], sp[1], ss[1], sp[2], ss[2], sp[3], ss[3],
            sp[4], ss[4], sp[5])
```

```python
import functools

import jax
import jax.numpy as jnp
from jax import lax
from jax.experimental import pallas as pl
from jax.experimental.pallas import tpu as pltpu

F32 = jnp.float32
I32 = jnp.int32
MXU_DTYPE = jnp.bfloat16
V7X_VMEM_LIMIT = 56 * 1024 * 1024

D_MODEL = 2048
DEPTH = 2
PAGE = 128
HD = 128
DSA_H, IDX_H, IDX_DIM, DSA_TOPK = 4, 16, 64, 256
DSA_ROW = 2 * HD + IDX_DIM
MOBA_H, MOBA_BLOCK, MOBA_TOPK = 4, 256, 3
SSM_H, SSM_P, SSM_G, SSM_N, SSM_CONV = 16, 64, 2, 128, 4
SSM_DI = SSM_H * SSM_P
CONV_DIM = SSM_DI + 2 * SSM_G * SSM_N
SSD_CHUNK = 256
FOX_H = 4
MEM_H = 4
N_EXPERTS, TOP_K, D_FF = 32, 4, 2048
SWIGLU_LIMIT, SWIGLU_ALPHA = 7.0, 1.702
ALPHA = (2 * DEPTH) ** 0.25
LN_EPS = 1e-5
RMS_EPS = 1e-5
IN_WIDTHS = (DSA_H * HD, 2 * HD, IDX_H * IDX_DIM, IDX_DIM, IDX_H, 3 * MOBA_H * HD,
             SSM_DI, CONV_DIM, SSM_H, 3 * FOX_H * HD, FOX_H)
SM_IW, SM_DT, SM_FF = 0, IDX_H, IDX_H + SSM_H
Q_IDX, Q_DSA, Q_MOBA, Q_FOX = 0, 1024, 1536, 2048
Q_WIDTH = 2560
MOE_ROWS = 512
NEG_INF = float("-inf")
INT_MIN = -2 ** 31


def _params(sem):
    return pltpu.CompilerParams(dimension_semantics=sem, vmem_limit_bytes=V7X_VMEM_LIMIT)


def _dot(a, b):
    return jnp.dot(a.astype(MXU_DTYPE), b.astype(MXU_DTYPE), preferred_element_type=F32)


def _dot_nt(a, b):
    return lax.dot_general(a.astype(MXU_DTYPE), b.astype(MXU_DTYPE), (((1,), (1,)), ((), ())),
                           preferred_element_type=F32)


def _dot_exact(a, b):
    return jnp.dot(a, b, preferred_element_type=F32, precision=lax.Precision.HIGHEST)


def _dot_nt_exact(a, b):
    return lax.dot_general(a, b, (((1,), (1,)), ((), ())), preferred_element_type=F32,
                           precision=lax.Precision.HIGHEST)


def _rounded(x):
    return x.astype(MXU_DTYPE).astype(F32)


def _sigmoid(x):
    return 1.0 / (1.0 + jnp.exp(-x))


def _softplus(x):
    return jnp.maximum(x, 0.0) + jnp.log1p(jnp.exp(-jnp.abs(x)))


def _log_sigmoid(x):
    return jnp.minimum(x, 0.0) - jnp.log1p(jnp.exp(-jnp.abs(x)))


def _silu(x):
    return x * _sigmoid(x)


def _pick(n, cands):
    for c in cands:
        if n % c == 0:
            return c
    return n


def _mm_kernel(x_ref, w_ref, o_ref):
    o_ref[...] = _dot(x_ref[...], w_ref[...]).astype(o_ref.dtype)


def mm(x, w, out_dtype=F32):
    m, k = x.shape
    n = w.shape[1]
    tm = _pick(m, (1024, 512, 256, 128))
    tn = _pick(n, (512, 384, 256, 128)) if n % 128 == 0 else n
    return pl.pallas_call(
        _mm_kernel,
        grid=(m // tm, n // tn),
        in_specs=[pl.BlockSpec((tm, k), lambda i, j: (i, 0)),
                  pl.BlockSpec((k, tn), lambda i, j: (0, j))],
        out_specs=pl.BlockSpec((tm, tn), lambda i, j: (i, j)),
        out_shape=jax.ShapeDtypeStruct((m, n), out_dtype),
        compiler_params=_params(("parallel", "arbitrary")),
        name="mm",
    )(x, w)


def _layer_norm(z, g, b):
    mu = jnp.mean(z, axis=-1, keepdims=True)
    zc = z - mu
    var = jnp.mean(zc * zc, axis=-1, keepdims=True)
    return zc * lax.rsqrt(var + LN_EPS) * g + b


def _mm_ln_kernel(a_ref, w_ref, r_ref, g_ref, b_ref, o_ref, ob_ref):
    z = ALPHA * r_ref[...] + _dot(a_ref[...], w_ref[...])
    y = _layer_norm(z, g_ref[...], b_ref[...])
    o_ref[...] = y
    ob_ref[...] = y.astype(ob_ref.dtype)


def mm_ln(a, w, resid, g, b):
    m, k = a.shape
    d = w.shape[1]
    tm = _pick(m, (512, 256, 128))
    return pl.pallas_call(
        _mm_ln_kernel,
        grid=(m // tm,),
        in_specs=[pl.BlockSpec((tm, k), lambda i: (i, 0)),
                  pl.BlockSpec((k, d), lambda i: (0, 0)),
                  pl.BlockSpec((tm, d), lambda i: (i, 0)),
                  pl.BlockSpec((1, d), lambda i: (0, 0)),
                  pl.BlockSpec((1, d), lambda i: (0, 0))],
        out_specs=[pl.BlockSpec((tm, d), lambda i: (i, 0)),
                   pl.BlockSpec((tm, d), lambda i: (i, 0))],
        out_shape=[jax.ShapeDtypeStruct((m, d), F32), jax.ShapeDtypeStruct((m, d), MXU_DTYPE)],
        compiler_params=_params(("parallel",)),
        name="mm_ln",
    )(a, w, resid, g.reshape(1, d), b.reshape(1, d))


def _gate_merge_kernel(x_ref, o0_ref, o1_ref, o2_ref, o3_ref, wg0_ref, wg1_ref, wg2_ref, wg3_ref,
                       bg0_ref, bg1_ref, bg2_ref, bg3_ref, p0_ref, p1_ref, p2_ref, p3_ref, m_ref):
    x = x_ref[...]
    acc = None
    for o_ref, wg_ref, bg_ref, p_ref in ((o0_ref, wg0_ref, bg0_ref, p0_ref), (o1_ref, wg1_ref, bg1_ref, p1_ref),
                                         (o2_ref, wg2_ref, bg2_ref, p2_ref), (o3_ref, wg3_ref, bg3_ref, p3_ref)):
        g = _sigmoid(_dot(x, wg_ref[...]) + bg_ref[...])
        t = g * _dot(o_ref[...], p_ref[...])
        acc = t if acc is None else acc + t
    m_ref[...] = acc.astype(m_ref.dtype)


def gate_merge(xb, outs, w_gate, b_gate, projs):
    m, d = xb.shape
    tm = _pick(m, (1024, 512, 256, 128))
    tn = 256
    nj = d // tn
    in_specs = [pl.BlockSpec((tm, d), lambda i, j: (i, 0))]
    in_specs += [pl.BlockSpec((tm, o.shape[1]), lambda i, j: (i, 0)) for o in outs]
    in_specs += [pl.BlockSpec((d, tn), functools.partial(lambda i, j, br: (0, br * nj + j), br=br)) for br in range(4)]
    in_specs += [pl.BlockSpec((1, tn), functools.partial(lambda i, j, br: (0, br * nj + j), br=br)) for br in range(4)]
    in_specs += [pl.BlockSpec((p.shape[0], tn), lambda i, j: (0, j)) for p in projs]
    bg = b_gate.reshape(1, 4 * d)
    return pl.pallas_call(
        _gate_merge_kernel,
        grid=(m // tm, nj),
        in_specs=in_specs,
        out_specs=pl.BlockSpec((tm, tn), lambda i, j: (i, j)),
        out_shape=jax.ShapeDtypeStruct((m, d), MXU_DTYPE),
        compiler_params=_params(("parallel", "arbitrary")),
        name="gate_merge",
    )(xb, *outs, w_gate, w_gate, w_gate, w_gate, bg, bg, bg, bg, *projs)


def _tri_incl(n):
    row = lax.broadcasted_iota(I32, (n, n), 0)
    col = lax.broadcasted_iota(I32, (n, n), 1)
    return col <= row


def _small_post_kernel(s_ref, bias_ref, post_ref, fcum_ref, carry_ref):
    c = pl.program_id(1)

    @pl.when(c == 0)
    def _():
        carry_ref[...] = jnp.zeros_like(carry_ref)

    raw = s_ref[...]
    z = raw + bias_ref[...]
    lane = lax.broadcasted_iota(I32, z.shape, 1)
    is_f = (lane >= SM_FF) & (lane < SM_FF + FOX_H)
    lf = jnp.where(is_f, _log_sigmoid(z), 0.0)
    post_ref[...] = jnp.where(lane < SM_DT, raw, jnp.where(lane < SM_FF, _softplus(z), lf))
    ck = z.shape[0]
    cs = _dot_exact(jnp.where(_tri_incl(ck), 1.0, 0.0), lf) + carry_ref[...]
    fcum_ref[...] = cs
    carry_ref[...] = cs[ck - 1:ck, :]


def small_post(small, bias_row, nb, t):
    ck = min(t, 256)
    nc = t // ck
    return pl.pallas_call(
        _small_post_kernel,
        grid=(nb, nc),
        in_specs=[pl.BlockSpec((ck, 128), lambda b, c: (b * nc + c, 0)),
                  pl.BlockSpec((1, 128), lambda b, c: (0, 0))],
        out_specs=[pl.BlockSpec((ck, 128), lambda b, c: (b * nc + c, 0)),
                   pl.BlockSpec((ck, 128), lambda b, c: (b * nc + c, 0))],
        out_shape=[jax.ShapeDtypeStruct(small.shape, F32), jax.ShapeDtypeStruct(small.shape, F32)],
        scratch_shapes=[pltpu.VMEM((1, 128), F32)],
        compiler_params=_params(("parallel", "arbitrary")),
        name="small_post",
    )(small, bias_row)


def _softmax_av(lg, v):
    m = jnp.max(lg, axis=-1, keepdims=True)
    p = jnp.exp(lg - m)
    return _dot(p / jnp.sum(p, axis=-1, keepdims=True), v)


def _sort_key(s):
    bits = lax.bitcast_convert_type(s + 0.0, I32)
    return bits ^ (lax.shift_right_arithmetic(bits, 31) & 0x7FFFFFFF)


def _kth_largest_key(count_ge, k, shape):
    kf = float(k)
    t0 = jnp.where(count_ge(jnp.zeros(shape, I32)) >= kf, 0, INT_MIN).astype(I32)

    def body(i, t):
        cand = t | lax.shift_left(jnp.int32(1), 30 - i)
        return jnp.where(count_ge(cand) >= kf, cand, t)

    return lax.fori_loop(0, 31, body, t0)


def _top_rank_mask(gsc, n_sel):
    nb = gsc.shape[1]
    blk = lax.broadcasted_iota(I32, gsc.shape, 1)
    rank = jnp.zeros(gsc.shape, F32)
    for m in range(nb):
        cm = gsc[:, m:m + 1]
        beats = jnp.where(cm > gsc, 1.0, jnp.where(cm == gsc, jnp.where(blk > m, 1.0, 0.0), 0.0))
        rank = rank + beats
    return jnp.where(rank < float(n_sel), jnp.where(gsc > NEG_INF, 1.0, 0.0), 0.0)


def _causal_groups(n_tiles):
    return min(8, n_tiles)


def _for_visible_keys(qi, n_tiles, tile, body):
    n_groups = _causal_groups(n_tiles)
    per = n_tiles // n_groups
    for g in range(n_groups):
        pl.when(qi // per == g)(functools.partial(body, (g + 1) * per * tile))


def _dsa_prompt_kernel(iq_ref, q_ref, sm_ref, row_ref, o_ref, *, n_top):
    qi = pl.program_id(1)
    tq = q_ref.shape[0]
    _for_visible_keys(qi, row_ref.shape[0] // tq, tq,
                      functools.partial(_dsa_prompt_body, iq_ref, q_ref, sm_ref, row_ref, o_ref, qi, n_top))


def _dsa_prompt_body(iq_ref, q_ref, sm_ref, row_ref, o_ref, qi, n_top, t):
    tq = q_ref.shape[0]
    row_ref = row_ref.at[0:t, :]
    kidx = row_ref[:, 2 * HD:2 * HD + IDX_DIM].astype(MXU_DTYPE)
    iq = iq_ref[...]
    iw = _rounded(sm_ref[:, SM_IW:SM_IW + IDX_H])
    score = jnp.zeros((tq, t), F32)
    for h in range(IDX_H):
        s = _dot_nt(iq[:, h * IDX_DIM:(h + 1) * IDX_DIM], kidx)
        score = score + _rounded(jnp.maximum(s, 0.0)) * iw[:, h:h + 1]
    tpos = qi * tq + lax.broadcasted_iota(I32, (tq, t), 0)
    lpos = lax.broadcasted_iota(I32, (tq, t), 1)
    causal = lpos <= tpos
    key = jnp.where(causal, _sort_key(score), INT_MIN)
    thr = _kth_largest_key(lambda c: jnp.sum(jnp.where(key >= c, 1.0, 0.0), axis=-1, keepdims=True),
                           n_top, (tq, 1))
    sel = key >= thr
    kb = row_ref[:, 0:HD].astype(MXU_DTYPE)
    vb = row_ref[:, HD:2 * HD].astype(MXU_DTYPE)
    for h in range(DSA_H):
        lg = _dot_nt(q_ref[:, h * HD:(h + 1) * HD], kb) * HD ** -0.5
        lg = jnp.where(causal, jnp.where(sel, lg, NEG_INF), NEG_INF)
        o_ref[:, h * HD:(h + 1) * HD] = _softmax_av(lg, vb).astype(o_ref.dtype)


def dsa_prompt(q_all, post, dsa_row, nb, t, n_top):
    tq = min(t, 256)
    nq = t // tq
    return pl.pallas_call(
        functools.partial(_dsa_prompt_kernel, n_top=n_top),
        grid=(nb, nq),
        in_specs=[pl.BlockSpec((tq, IDX_H * IDX_DIM), lambda b, i: (b * nq + i, Q_IDX // 1024)),
                  pl.BlockSpec((tq, DSA_H * HD), lambda b, i: (b * nq + i, Q_DSA // 512)),
                  pl.BlockSpec((tq, 128), lambda b, i: (b * nq + i, 0)),
                  pl.BlockSpec((t, DSA_ROW), lambda b, i: (b, 0))],
        out_specs=pl.BlockSpec((tq, DSA_H * HD), lambda b, i: (b * nq + i, 0)),
        out_shape=jax.ShapeDtypeStruct((nb * t, DSA_H * HD), MXU_DTYPE),
        compiler_params=_params(("parallel", "arbitrary")),
        name="dsa_prompt",
    )(q_all, q_all, post, dsa_row)


def _moba_prompt_kernel(q_ref, kv_ref, o_ref, kb_ref, vb_ref, mean_ref, *, n_sel):
    qi = pl.program_id(2)
    t = kv_ref.shape[0]
    nblk = t // MOBA_BLOCK

    @pl.when(qi == 0)
    def _():
        kb_ref[...] = kv_ref[:, 0:HD].astype(MXU_DTYPE)
        vb_ref[...] = kv_ref[:, HD:2 * HD].astype(MXU_DTYPE)
        for n in range(nblk):
            mean_ref[n:n + 1, :] = jnp.mean(kv_ref[n * MOBA_BLOCK:(n + 1) * MOBA_BLOCK, 0:HD], axis=0,
                                            keepdims=True)

    _for_visible_keys(qi, nblk, MOBA_BLOCK,
                      functools.partial(_moba_prompt_body, q_ref, o_ref, kb_ref, vb_ref, mean_ref, qi, n_sel))


def _moba_prompt_body(q_ref, o_ref, kb_ref, vb_ref, mean_ref, qi, n_sel, t):
    nblk = t // MOBA_BLOCK
    q = q_ref[...]
    logits = _dot_nt(q, kb_ref[0:t, :]) * HD ** -0.5
    tq = q.shape[0]
    own_causal = (lax.broadcasted_iota(I32, (tq, MOBA_BLOCK), 1) <= lax.broadcasted_iota(I32, (tq, MOBA_BLOCK), 0))
    if n_sel:
        gsc = _dot_nt(q, mean_ref[...])
        gsc = jnp.where(lax.broadcasted_iota(I32, gsc.shape, 1) < qi, gsc, NEG_INF)
        selm = _top_rank_mask(gsc, n_sel)
    pieces = []
    for n in range(nblk):
        lg = logits[:, n * MOBA_BLOCK:(n + 1) * MOBA_BLOCK]
        piece = jnp.where(qi == n, jnp.where(own_causal, lg, NEG_INF), NEG_INF)
        if n_sel:
            piece = jnp.where(selm[:, n:n + 1] > 0.5, lg, piece)
        pieces.append(piece)
    lg = jnp.concatenate(pieces, axis=1) if nblk > 1 else pieces[0]
    o_ref[...] = _softmax_av(lg, vb_ref[0:t, :]).astype(o_ref.dtype)


def moba_prompt(q_all, moba_row, nb, t, n_sel):
    nq = t // MOBA_BLOCK
    return pl.pallas_call(
        functools.partial(_moba_prompt_kernel, n_sel=n_sel),
        grid=(nb, MOBA_H, nq),
        in_specs=[pl.BlockSpec((MOBA_BLOCK, HD), lambda b, h, i: (b * nq + i, Q_MOBA // HD + h)),
                  pl.BlockSpec((t, 2 * HD), lambda b, h, i: (b, h))],
        out_specs=pl.BlockSpec((MOBA_BLOCK, HD), lambda b, h, i: (b * nq + i, h)),
        out_shape=jax.ShapeDtypeStruct((nb * t, MOBA_H * HD), MXU_DTYPE),
        scratch_shapes=[pltpu.VMEM((t, HD), MXU_DTYPE), pltpu.VMEM((t, HD), MXU_DTYPE),
                        pltpu.VMEM((nq, HD), F32)],
        compiler_params=_params(("parallel", "parallel", "arbitrary")),
        name="moba_prompt",
    )(q_all, moba_row)


def _fox_prompt_kernel(q_ref, kv_ref, fc_ref, fk_ref, o_ref, kb_ref, vb_ref):
    h = pl.program_id(1)
    qi = pl.program_id(2)

    @pl.when(qi == 0)
    def _():
        kb_ref[...] = kv_ref[:, 0:HD].astype(MXU_DTYPE)
        vb_ref[...] = kv_ref[:, HD:2 * HD].astype(MXU_DTYPE)

    tq = q_ref.shape[0]
    _for_visible_keys(qi, kv_ref.shape[0] // tq, tq,
                      functools.partial(_fox_prompt_body, q_ref, fc_ref, fk_ref, o_ref, kb_ref, vb_ref, h, qi))


def _fox_prompt_body(q_ref, fc_ref, fk_ref, o_ref, kb_ref, vb_ref, h, qi, t):
    tq = q_ref.shape[0]
    fc = fc_ref[...]
    fq = jnp.sum(jnp.where(lax.broadcasted_iota(I32, fc.shape, 1) == SM_FF + h, fc, 0.0), axis=-1, keepdims=True)
    lg = _dot_nt(q_ref[...], kb_ref[0:t, :]) * HD ** -0.5
    lg = lg + fq - fk_ref[:, 0:t]
    causal = lax.broadcasted_iota(I32, (tq, t), 1) <= qi * tq + lax.broadcasted_iota(I32, (tq, t), 0)
    lg = jnp.where(causal, lg, NEG_INF)
    o_ref[...] = _softmax_av(lg, vb_ref[0:t, :]).astype(o_ref.dtype)


def fox_prompt(q_all, fox_kv, fcum, fk_rows, nb, t):
    tq = min(t, 256)
    nq = t // tq
    return pl.pallas_call(
        _fox_prompt_kernel,
        grid=(nb, FOX_H, nq),
        in_specs=[pl.BlockSpec((tq, HD), lambda b, h, i: (b * nq + i, Q_FOX // HD + h)),
                  pl.BlockSpec((t, 2 * HD), lambda b, h, i: (b, h)),
                  pl.BlockSpec((tq, 128), lambda b, h, i: (b * nq + i, 0)),
                  pl.BlockSpec((None, 1, t), lambda b, h, i: (b * FOX_H + h, 0, 0))],
        out_specs=pl.BlockSpec((tq, HD), lambda b, h, i: (b * nq + i, h)),
        out_shape=jax.ShapeDtypeStruct((nb * t, FOX_H * HD), MXU_DTYPE),
        scratch_shapes=[pltpu.VMEM((t, HD), MXU_DTYPE), pltpu.VMEM((t, HD), MXU_DTYPE)],
        compiler_params=_params(("parallel", "parallel", "arbitrary")),
        name="fox_prompt",
    )(q_all, fox_kv, fcum, fk_rows)


def _gated_group_norm(y, z, ng):
    y = y * _silu(z)
    gw = SSM_DI // SSM_G
    parts = []
    for g in range(SSM_G):
        yg = y[:, g * gw:(g + 1) * gw]
        parts.append(yg * lax.rsqrt(jnp.mean(yg * yg, axis=-1, keepdims=True) + RMS_EPS))
    return jnp.concatenate(parts, axis=1) * ng


def _chunk_decays(alast_ref, z, nc):
    rows = []
    for j in range(nc):
        e = jnp.zeros((1, 128), F32)
        for m in range(j + 1, nc):
            e = e + jnp.where(m < z, alast_ref[m], 0.0)
        rows.append(jnp.where(j < z, _rounded(jnp.exp(e)), 0.0))
    return rows


def _carried_state(dec, sloc_ref, h):
    acc = None
    for j, d in enumerate(dec):
        t = d[:, SM_DT + h:SM_DT + h + 1] * sloc_ref[j, h].astype(F32)
        acc = t if acc is None else acc + t
    return acc


def _ssd_prompt_kernel(xbc_ref, z_ref, post_ref, cw_ref, cb_ref, hp_ref, ng_ref, o_ref, st_ref,
                       sloc_ref, alast_ref, tail_ref):
    c = pl.program_id(1)
    nc = sloc_ref.shape[0]
    ln = xbc_ref.shape[0]

    @pl.when(c == 0)
    def _():
        sloc_ref[...] = jnp.zeros_like(sloc_ref)
        alast_ref[...] = jnp.zeros_like(alast_ref)
        tail_ref[...] = jnp.zeros_like(tail_ref)

    xin = xbc_ref[...]
    xx = _rounded(jnp.concatenate([tail_ref[...], xin], axis=0))
    cw = _rounded(cw_ref[...])
    conv = cw[0:1, :] * xx[5:5 + ln, :]
    for j in range(1, SSM_CONV):
        conv = conv + cw[j:j + 1, :] * xx[5 + j:5 + j + ln, :]
    conv = conv + cb_ref[...]
    tail_ref[...] = xin[ln - 8:ln, :]
    u = _silu(conv)
    xs = u[:, :SSM_DI]
    xs_t = xs.T
    lane = lax.broadcasted_iota(I32, (ln, 128), 1)
    is_dt = (lane >= SM_DT) & (lane < SM_DT + SSM_H)
    dt = jnp.where(is_dt, post_ref[...], 0.0)
    a_row = -jnp.exp(hp_ref[1:2, :])
    adt = jnp.where(is_dt, dt * a_row, 0.0)
    low = _tri_incl(ln)
    acs = _dot_exact(jnp.where(low, 1.0, 0.0), adt)
    acs_t = acs.T
    dt_t = dt.T
    a_last = acs[ln - 1:ln, :]
    dec = _chunk_decays(alast_ref, c, nc)
    ys = []
    for g in range(SSM_G):
        bg = u[:, SSM_DI + g * SSM_N:SSM_DI + (g + 1) * SSM_N]
        cg = u[:, SSM_DI + SSM_G * SSM_N + g * SSM_N:SSM_DI + SSM_G * SSM_N + (g + 1) * SSM_N]
        cb = _dot_nt(cg, bg)
        for hh in range(SSM_H // SSM_G):
            h = g * (SSM_H // SSM_G) + hh
            col = acs[:, SM_DT + h:SM_DT + h + 1]
            row = acs_t[SM_DT + h:SM_DT + h + 1, :]
            last = a_last[:, SM_DT + h:SM_DT + h + 1]
            x_h = xs[:, h * SSM_P:(h + 1) * SSM_P]
            xdt = x_h * dt[:, SM_DT + h:SM_DT + h + 1]
            scores = cb * jnp.where(low, jnp.exp(col - row), 0.0)
            y = _dot(scores, xdt) + jnp.exp(col) * _dot_nt(cg, _carried_state(dec, sloc_ref, h))
            ys.append(y + x_h * hp_ref[2:3, SM_DT + h:SM_DT + h + 1])
            xdt_t = (xs_t[h * SSM_P:(h + 1) * SSM_P, :] * dt_t[SM_DT + h:SM_DT + h + 1, :]) * jnp.exp(last - row)
            sloc_ref[c, h] = _dot(xdt_t, bg).astype(sloc_ref.dtype)
    alast_ref[c] = a_last
    y = jnp.concatenate(ys, axis=1)
    o_ref[...] = _gated_group_norm(y, z_ref[...], ng_ref[...]).astype(o_ref.dtype)

    @pl.when(c == nc - 1)
    def _():
        dec_end = _chunk_decays(alast_ref, nc, nc)
        for h in range(SSM_H):
            st_ref[h] = _carried_state(dec_end, sloc_ref, h)


def ssd_prompt(xbc, z, post, conv_w, conv_b, head_params, norm_g, nb, t):
    ck = min(t, SSD_CHUNK)
    nc = t // ck
    return pl.pallas_call(
        _ssd_prompt_kernel,
        grid=(nb, nc),
        in_specs=[pl.BlockSpec((ck, CONV_DIM), lambda b, c: (b * nc + c, 0)),
                  pl.BlockSpec((ck, SSM_DI), lambda b, c: (b * nc + c, 0)),
                  pl.BlockSpec((ck, 128), lambda b, c: (b * nc + c, 0)),
                  pl.BlockSpec((SSM_CONV, CONV_DIM), lambda b, c: (0, 0)),
                  pl.BlockSpec((1, CONV_DIM), lambda b, c: (0, 0)),
                  pl.BlockSpec((8, 128), lambda b, c: (0, 0)),
                  pl.BlockSpec((1, SSM_DI), lambda b, c: (0, 0))],
        out_specs=[pl.BlockSpec((ck, SSM_DI), lambda b, c: (b * nc + c, 0)),
                   pl.BlockSpec((None, SSM_H, SSM_P, SSM_N), lambda b, c: (b, 0, 0, 0))],
        out_shape=[jax.ShapeDtypeStruct((nb * t, SSM_DI), MXU_DTYPE),
                   jax.ShapeDtypeStruct((nb, SSM_H, SSM_P, SSM_N), F32)],
        scratch_shapes=[pltpu.VMEM((nc, SSM_H, SSM_P, SSM_N), MXU_DTYPE), pltpu.VMEM((nc, 1, 128), F32),
                        pltpu.VMEM((8, CONV_DIM), F32)],
        compiler_params=_params(("parallel", "arbitrary")),
        name="ssd_prompt",
    )(xbc, z, post, conv_w, conv_b.reshape(1, CONV_DIM), head_params, norm_g.reshape(1, SSM_DI))


def _mem_prompt_kernel(q_ref, kv_ref, o_ref):
    for h in range(MEM_H):
        k = kv_ref[:, h * 2 * HD:h * 2 * HD + HD]
        v = kv_ref[:, h * 2 * HD + HD:(h + 1) * 2 * HD]
        lg = _dot_nt(q_ref[:, h * HD:(h + 1) * HD], k) * HD ** -0.5
        o_ref[:, h * HD:(h + 1) * HD] = _softmax_av(lg, v).astype(o_ref.dtype)


def mem_prompt_attn(mq, mem_kv, nb, t, mem_len):
    tq = _pick(t, (512, 256, 128))
    nq = t // tq
    return pl.pallas_call(
        _mem_prompt_kernel,
        grid=(nb, nq),
        in_specs=[pl.BlockSpec((tq, MEM_H * HD), lambda b, i: (b * nq + i, 0)),
                  pl.BlockSpec((mem_len, MEM_H * 2 * HD), lambda b, i: (b, 0))],
        out_specs=pl.BlockSpec((tq, MEM_H * HD), lambda b, i: (b * nq + i, 0)),
        out_shape=jax.ShapeDtypeStruct((nb * t, MEM_H * HD), MXU_DTYPE),
        compiler_params=_params(("parallel", "arbitrary")),
        name="mem_prompt",
    )(mq, mem_kv)


def _split_cols(w, widths):
    out, o = [], 0
    for wd in widths:
        out.append(w[:, o:o + wd])
        o += wd
    return out


def _interleave_heads(k, v, n_heads):
    pieces = []
    for h in range(n_heads):
        pieces += [k[:, h * HD:(h + 1) * HD], v[:, h * HD:(h + 1) * HD]]
    return jnp.concatenate(pieces, axis=1)


def prep_layer(l, p):
    c = lambda a: a.astype(MXU_DTYPE)
    (dsa_q, dsa_kv, idx_q, idx_k, idx_w, moba_qkv, ssm_z, ssm_xbc, ssm_dt, fox_qkv, fox_f) = _split_cols(
        p["w_in"][l], IN_WIDTHS)
    moba_q, moba_k, moba_v = _split_cols(moba_qkv, (MOBA_H * HD,) * 3)
    fox_q, fox_k, fox_v = _split_cols(fox_qkv, (FOX_H * HD,) * 3)
    pad = jnp.zeros((D_MODEL, 128 - SM_FF - FOX_H), F32)
    small_bias = jnp.zeros((1, 128), F32)
    small_bias = small_bias.at[0, SM_DT:SM_DT + SSM_H].set(p["dt_bias"][l])
    small_bias = small_bias.at[0, SM_FF:SM_FF + FOX_H].set(p["fox_b_f"][l])
    head_params = jnp.zeros((8, 128), F32)
    head_params = head_params.at[1, SM_DT:SM_DT + SSM_H].set(p["a_log"][l])
    head_params = head_params.at[2, SM_DT:SM_DT + SSM_H].set(p["d_skip"][l])
    router_pad = jnp.zeros((D_MODEL, 128 - N_EXPERTS), F32)
    return dict(
        w_q=c(jnp.concatenate([idx_q * IDX_DIM ** -0.5, dsa_q, moba_q, fox_q], axis=1)),
        w_dsa_row=c(jnp.concatenate([dsa_kv, idx_k], axis=1)),
        w_moba_row=c(_interleave_heads(moba_k, moba_v, MOBA_H)),
        w_fox_kv=c(_interleave_heads(fox_k, fox_v, FOX_H)),
        w_z=c(ssm_z), w_xbc=c(ssm_xbc),
        w_small=c(jnp.concatenate([idx_w * IDX_H ** -0.5, ssm_dt, fox_f, pad], axis=1)),
        small_bias=small_bias, head_params=head_params,
        conv_w=p["conv_w"][l], conv_b=p["conv_b"][l], norm_g=p["ssm_norm_g"][l],
        projs=[c(p["w_br_dsa"][l]), c(p["w_br_moba"][l]), c(p["w_br_ssm"][l]), c(p["w_br_fox"][l])],
        w_gate=c(p["w_gate"][l]), b_gate=p["b_gate"][l], w_o=c(p["w_o"][l]),
        ln1=(p["ln1_g"][l], p["ln1_b"][l]), ln2=(p["ln2_g"][l], p["ln2_b"][l]), ln3=(p["ln3_g"][l], p["ln3_b"][l]),
        w_mq=c(p["w_mq"][l]), w_mkv=c(p["w_mkv"][l]), w_mo=c(p["w_mo"][l]),
        w_router=c(jnp.concatenate([p["w_router"][l], router_pad], axis=1)),
        b_router=jnp.concatenate([p["b_router"][l], jnp.full((128 - N_EXPERTS,), NEG_INF, F32)]).reshape(1, 128),
        w1=p["w1"], b1=p["b1"], w2=p["w2"], b2=p["b2"], layer=l,
    )


def in_proj(xb, lw, q_dtype=None):
    return dict(
        q_all=mm(xb, lw["w_q"], q_dtype or MXU_DTYPE),
        dsa_row=mm(xb, lw["w_dsa_row"]),
        moba_row=mm(xb, lw["w_moba_row"]),
        fox_kv=mm(xb, lw["w_fox_kv"]),
        z=mm(xb, lw["w_z"]),
        xbc=mm(xb, lw["w_xbc"]),
        small=mm(xb, lw["w_small"]),
    )


def mixer_prompt(xb, lw, nb, t):
    pr = in_proj(xb, lw)
    post, fcum = small_post(pr["small"], lw["small_bias"], nb, t)
    o_dsa = dsa_prompt(pr["q_all"], post, pr["dsa_row"], nb, t, min(DSA_TOPK, t // 4))
    o_moba = moba_prompt(pr["q_all"], pr["moba_row"], nb, t, min(MOBA_TOPK, (t - 1) // MOBA_BLOCK))
    o_ssm, ssm_new = ssd_prompt(pr["xbc"], pr["z"], post, lw["conv_w"], lw["conv_b"], lw["head_params"],
                                lw["norm_g"], nb, t)
    fk_rows = fcum[:, SM_FF:SM_FF + FOX_H].reshape(nb, t, FOX_H).transpose(0, 2, 1).reshape(nb * FOX_H, 1, t)
    o_fox = fox_prompt(pr["q_all"], pr["fox_kv"], fcum, fk_rows, nb, t)
    logf = post[:, SM_FF:SM_FF + FOX_H].reshape(nb, t, FOX_H, 1)
    rows = (pr["dsa_row"].reshape(nb, t, DSA_ROW),
            pr["moba_row"].reshape(nb, t, MOBA_H, 2 * HD),
            jnp.concatenate([pr["fox_kv"].reshape(nb, t, FOX_H, 2 * HD), logf], axis=-1),
            ssm_new,
            pr["xbc"].reshape(nb, t, CONV_DIM)[:, t - (SSM_CONV - 1):])
    return (o_dsa, o_moba, o_ssm, o_fox), rows


MOE_TOK = 128


def _router_kernel(h_ref, w_ref, b_ref, idx_ref, gate_ref, cnt_ref, carry_ref):
    i = pl.program_id(0)

    @pl.when(i == 0)
    def _():
        carry_ref[...] = jnp.zeros_like(carry_ref)

    work = _dot(h_ref[...], w_ref[...]) + b_ref[...]
    tm = work.shape[0]
    lane = lax.broadcasted_iota(I32, work.shape, 1)
    hot = jnp.zeros(work.shape, F32)
    vals, idxs = [], []
    for _k in range(TOP_K):
        m = jnp.max(work, axis=-1, keepdims=True)
        idx = jnp.min(jnp.where(work == m, lane, 128), axis=-1, keepdims=True)
        vals.append(m)
        idxs.append(idx)
        hot = hot + jnp.where(lane == idx, 1.0, 0.0)
        work = jnp.where(lane == idx, NEG_INF, work)
    es = [jnp.exp(v - vals[0]) for v in vals]
    tot = es[0] + es[1] + es[2] + es[3]
    strict = lax.broadcasted_iota(I32, (tm, tm), 1) < lax.broadcasted_iota(I32, (tm, tm), 0)
    prefix = _dot(jnp.where(strict, 1.0, 0.0), hot) + carry_ref[...]
    out_i = jnp.zeros(work.shape, I32)
    out_g = jnp.zeros(work.shape, F32)
    for k in range(TOP_K):
        rank = jnp.sum(jnp.where(lane == idxs[k], prefix, 0.0), axis=-1, keepdims=True)
        out_i = jnp.where(lane == k, idxs[k], out_i)
        out_i = jnp.where(lane == TOP_K + k, rank.astype(I32), out_i)
        out_g = jnp.where(lane == k, es[k] / tot, out_g)
    idx_ref[...] = out_i
    gate_ref[...] = out_g
    carry_ref[...] = carry_ref[...] + jnp.sum(hot, axis=0, keepdims=True)
    cnt_ref[...] = carry_ref[...]


def router(hb, w_router, b_router):
    n, d = hb.shape
    tm = MOE_TOK
    return pl.pallas_call(
        _router_kernel,
        grid=(n // tm,),
        in_specs=[pl.BlockSpec((tm, d), lambda i: (i, 0)),
                  pl.BlockSpec((d, 128), lambda i: (0, 0)),
                  pl.BlockSpec((1, 128), lambda i: (0, 0))],
        out_specs=[pl.BlockSpec((tm, 128), lambda i: (i, 0)),
                   pl.BlockSpec((tm, 128), lambda i: (i, 0)),
                   pl.BlockSpec((1, 128), lambda i: (0, 0))],
        out_shape=[jax.ShapeDtypeStruct((n, 128), I32), jax.ShapeDtypeStruct((n, 128), F32),
                   jax.ShapeDtypeStruct((1, 128), F32)],
        scratch_shapes=[pltpu.VMEM((1, 128), F32)],
        compiler_params=_params(("arbitrary",)),
        name="router",
    )(hb, w_router, b_router)


def _row_copy(src, dst, sem):
    return pltpu.make_async_copy(src, dst, sem)


def _pack_pairs(x):
    half = x.shape[1] // 2
    hi = lax.bitcast_convert_type(_rounded(x[:, :half]), I32) & (-65536)
    lo = lax.shift_right_logical(lax.bitcast_convert_type(_rounded(x[:, half:]), I32), 16)
    return hi | lo


def _unpack_pairs(u):
    hi = lax.bitcast_convert_type(u & (-65536), F32)
    lo = lax.bitcast_convert_type(lax.shift_left(u, 16), F32)
    return jnp.concatenate([hi, lo], axis=1)


def _dispatch_kernel(dest_ref, x_ref, xs_in_ref, xs_ref, pk_ref, sem):
    del xs_in_ref
    n = x_ref.shape[0]
    pk_ref[...] = _pack_pairs(x_ref[...])

    def issue(r, carry):
        for k in range(TOP_K):
            d = dest_ref[0, r * TOP_K + k]
            _row_copy(pk_ref.at[pl.ds(r, 1), :], xs_ref.at[pl.ds(d, 1), :], sem).start()
        return carry

    lax.fori_loop(0, n, issue, 0)

    def drain(r, carry):
        for _k in range(TOP_K):
            _row_copy(pk_ref.at[pl.ds(0, 1), :], xs_ref.at[pl.ds(0, 1), :], sem).wait()
        return carry

    lax.fori_loop(0, n, drain, 0)


def dispatch(dest3, x, n_rows):
    n, d = x.shape
    xs0 = jnp.zeros((n_rows, d // 2), I32)
    return pl.pallas_call(
        _dispatch_kernel,
        grid=(n // MOE_TOK,),
        in_specs=[pl.BlockSpec((None, 1, MOE_TOK * TOP_K), lambda i: (i, 0, 0), memory_space=pltpu.SMEM),
                  pl.BlockSpec((MOE_TOK, d), lambda i: (i, 0)),
                  pl.BlockSpec(memory_space=pl.ANY)],
        out_specs=pl.BlockSpec(memory_space=pl.ANY),
        out_shape=jax.ShapeDtypeStruct((n_rows, d // 2), I32),
        scratch_shapes=[pltpu.VMEM((MOE_TOK, d // 2), I32), pltpu.SemaphoreType.DMA(())],
        input_output_aliases={2: 0},
        compiler_params=_params(("arbitrary",)),
        name="moe_dispatch",
    )(dest3, x, xs0)


def _gmm1_kernel(be_ref, bf_ref, xs_ref, wg_ref, wl_ref, bg_ref, bl_ref, o_ref, wgb_ref, wlb_ref):
    del be_ref
    i = pl.program_id(1)
    flags = bf_ref[i]

    @pl.when((flags & 2) != 0)
    def _():
        wgb_ref[...] = wg_ref[...].astype(wgb_ref.dtype)
        wlb_ref[...] = wl_ref[...].astype(wlb_ref.dtype)

    @pl.when((flags & 1) != 0)
    def _():
        x = _unpack_pairs(xs_ref[...]).astype(MXU_DTYPE)
        g = jnp.minimum(_dot(x, wgb_ref[...]) + bg_ref[...], SWIGLU_LIMIT)
        lin = jnp.clip(_dot(x, wlb_ref[...]) + bl_ref[...], -SWIGLU_LIMIT, SWIGLU_LIMIT)
        o_ref[...] = (g * _sigmoid(SWIGLU_ALPHA * g) * (lin + 1.0)).astype(o_ref.dtype)

    @pl.when((flags & 1) == 0)
    def _():
        o_ref[...] = jnp.zeros_like(o_ref)


def gmm1(blk_e, blk_flags, xs, w1, b1, layer):
    s, d_packed = xs.shape
    d = w1.shape[2]
    nblk = s // MOE_ROWS
    tn = 512
    nj = D_FF // tn
    gs = pltpu.PrefetchScalarGridSpec(
        num_scalar_prefetch=2, grid=(nj, nblk),
        in_specs=[pl.BlockSpec((MOE_ROWS, d_packed), lambda j, i, be, bf: (i, 0)),
                  pl.BlockSpec((None, None, d, tn), lambda j, i, be, bf: (layer, be[i], 0, j)),
                  pl.BlockSpec((None, None, d, tn), lambda j, i, be, bf: (layer, be[i], 0, nj + j)),
                  pl.BlockSpec((None, None, 1, tn), lambda j, i, be, bf: (layer, be[i], 0, j)),
                  pl.BlockSpec((None, None, 1, tn), lambda j, i, be, bf: (layer, be[i], 0, nj + j))],
        out_specs=pl.BlockSpec((MOE_ROWS, tn), lambda j, i, be, bf: (i, j)),
        scratch_shapes=[pltpu.VMEM((d, tn), MXU_DTYPE), pltpu.VMEM((d, tn), MXU_DTYPE)])
    b1r = b1.reshape(DEPTH, N_EXPERTS, 1, 2 * D_FF)
    return pl.pallas_call(
        _gmm1_kernel, grid_spec=gs,
        out_shape=jax.ShapeDtypeStruct((s, D_FF), MXU_DTYPE),
        compiler_params=_params(("arbitrary", "arbitrary")),
        name="moe_gmm1",
    )(blk_e, blk_flags, xs, w1, w1, b1r, b1r)


def _gmm2_kernel(be_ref, bf_ref, h_ref, w_ref, b_ref, o_ref, wb_ref):
    del be_ref
    i = pl.program_id(1)
    flags = bf_ref[i]

    @pl.when((flags & 2) != 0)
    def _():
        wb_ref[...] = w_ref[...].astype(wb_ref.dtype)

    @pl.when((flags & 1) != 0)
    def _():
        o_ref[...] = _dot(h_ref[...], wb_ref[...]) + b_ref[...]

    @pl.when((flags & 1) == 0)
    def _():
        o_ref[...] = jnp.zeros_like(o_ref)


def gmm2(blk_e, blk_flags, hs, w2, b2, layer):
    s, f = hs.shape
    d = w2.shape[3]
    nblk = s // MOE_ROWS
    tn = 1024
    nj = d // tn
    gs = pltpu.PrefetchScalarGridSpec(
        num_scalar_prefetch=2, grid=(nj, nblk),
        in_specs=[pl.BlockSpec((MOE_ROWS, f), lambda j, i, be, bf: (i, 0)),
                  pl.BlockSpec((None, None, f, tn), lambda j, i, be, bf: (layer, be[i], 0, j)),
                  pl.BlockSpec((None, None, 1, tn), lambda j, i, be, bf: (layer, be[i], 0, j))],
        out_specs=pl.BlockSpec((MOE_ROWS, tn), lambda j, i, be, bf: (i, j)),
        scratch_shapes=[pltpu.VMEM((f, tn), MXU_DTYPE)])
    return pl.pallas_call(
        _gmm2_kernel, grid_spec=gs,
        out_shape=jax.ShapeDtypeStruct((s, d), F32),
        compiler_params=_params(("arbitrary", "arbitrary")),
        name="moe_gmm2",
    )(blk_e, blk_flags, hs, w2, b2.reshape(DEPTH, N_EXPERTS, 1, d))


def _combine_kernel(dcur_ref, dnxt_ref, gate_ref, h_ref, g_ref, b_ref, ys_ref, o_ref, ob_ref, buf_ref, sem):
    i = pl.program_id(0)
    n_steps = pl.num_programs(0)
    n = h_ref.shape[0]
    slot = lax.rem(i, 2)

    def issue(d_ref, s):
        def body(r, carry):
            for k in range(TOP_K):
                d = d_ref[0, r * TOP_K + k]
                _row_copy(ys_ref.at[pl.ds(d, 1), :], buf_ref.at[s, k, pl.ds(r, 1), :], sem.at[s]).start()
            return carry
        lax.fori_loop(0, n, body, 0)

    @pl.when(i == 0)
    def _():
        issue(dcur_ref, 0)

    @pl.when(i + 1 < n_steps)
    def _():
        issue(dnxt_ref, 1 - slot)

    def drain(r, carry):
        for k in range(TOP_K):
            _row_copy(ys_ref.at[pl.ds(0, 1), :], buf_ref.at[slot, k, pl.ds(0, 1), :], sem.at[slot]).wait()
        return carry

    lax.fori_loop(0, n, drain, 0)
    gate = gate_ref[...]
    moe = gate[:, 0:1] * buf_ref[slot, 0]
    for k in range(1, TOP_K):
        moe = moe + gate[:, k:k + 1] * buf_ref[slot, k]
    y = _layer_norm(ALPHA * h_ref[...] + moe, g_ref[...], b_ref[...])
    o_ref[...] = y
    ob_ref[...] = y.astype(ob_ref.dtype)


def combine_ln(dest3, gate, h, ys, g, b):
    n, d = h.shape
    nt = n // MOE_TOK
    smem = lambda f: pl.BlockSpec((None, 1, MOE_TOK * TOP_K), f, memory_space=pltpu.SMEM)
    return pl.pallas_call(
        _combine_kernel,
        grid=(nt,),
        in_specs=[smem(lambda i: (i, 0, 0)),
                  smem(lambda i: (jnp.minimum(i + 1, nt - 1), 0, 0)),
                  pl.BlockSpec((MOE_TOK, 128), lambda i: (i, 0)),
                  pl.BlockSpec((MOE_TOK, d), lambda i: (i, 0)),
                  pl.BlockSpec((1, d), lambda i: (0, 0)),
                  pl.BlockSpec((1, d), lambda i: (0, 0)),
                  pl.BlockSpec(memory_space=pl.ANY)],
        out_specs=[pl.BlockSpec((MOE_TOK, d), lambda i: (i, 0)),
                   pl.BlockSpec((MOE_TOK, d), lambda i: (i, 0))],
        out_shape=[jax.ShapeDtypeStruct((n, d), F32), jax.ShapeDtypeStruct((n, d), MXU_DTYPE)],
        scratch_shapes=[pltpu.VMEM((2, TOP_K, MOE_TOK, d), F32), pltpu.SemaphoreType.DMA((2,))],
        compiler_params=_params(("arbitrary",)),
        name="moe_combine",
    )(dest3, dest3, gate, h, g.reshape(1, d), b.reshape(1, d), ys)


def moe_ln(h, hb, lw):
    n, d = h.shape
    idx, gate, cnt = router(hb, lw["w_router"], lw["b_router"])
    counts = cnt[0, :N_EXPERTS].astype(I32)
    padded = (counts + MOE_ROWS - 1) // MOE_ROWS * MOE_ROWS
    pad_end = jnp.cumsum(padded)
    pad_start = pad_end - padded
    dest = pad_start[idx[:, :TOP_K]] + idx[:, TOP_K:2 * TOP_K]
    dest3 = dest.reshape(n // MOE_TOK, 1, MOE_TOK * TOP_K)
    nblk = -(-(n * TOP_K) // MOE_ROWS) + N_EXPERTS
    blk_start = jnp.arange(nblk, dtype=I32) * MOE_ROWS
    blk_e = jnp.minimum(jnp.sum((blk_start[:, None] >= pad_end[None, :]).astype(I32), axis=1), N_EXPERTS - 1)
    valid = (blk_start < pad_end[-1]).astype(I32)
    first = jnp.concatenate([jnp.ones((1,), I32), (blk_e[1:] != blk_e[:-1]).astype(I32)])
    blk_flags = valid + 2 * first
    xs = dispatch(dest3, h, nblk * MOE_ROWS)
    hs = gmm1(blk_e, blk_flags, xs, lw["w1"], lw["b1"], lw["layer"])
    ys = gmm2(blk_e, blk_flags, hs, lw["w2"], lw["b2"], lw["layer"])
    return combine_ln(dest3, gate, h, ys, *lw["ln3"])


def _rows8(x):
    r = x.shape[0]
    return x if r == 8 else jnp.concatenate([x, jnp.zeros((8 - r, x.shape[1]), x.dtype)], axis=0)


def _paged_attend(lgs, lg_new, vs, v_new, exact=False, v_t=False):
    r = lg_new.shape[0]
    rnd = (lambda x: x) if exact else _rounded
    if v_t:
        pv = _dot_nt_exact if exact else _dot_nt
    else:
        pv = _dot_exact if exact else _dot
    m = lg_new
    for lg in lgs:
        m = jnp.maximum(m, jnp.max(lg, axis=-1, keepdims=True))
    p_new = jnp.exp(lg_new - m)
    ps = [jnp.exp(lg - m) for lg in lgs]
    tot = p_new
    for p in ps:
        tot = tot + jnp.sum(p, axis=-1, keepdims=True)
    acc = rnd(p_new / tot) * rnd(v_new)
    for p, v in zip(ps, vs):
        acc = acc + pv(_rows8(p / tot), v)[0:r, :]
    return acc


def _page_specs(shape_tail, layer, n_pages):
    nd = len(shape_tail)
    return [pl.BlockSpec((None, None) + shape_tail,
                         functools.partial(lambda b, pt, pg: (layer, pt[b, pg]) + (0,) * nd, pg=pg))
            for pg in range(n_pages)]


def _dsa_dec_score_kernel(pt_ref, *refs, n_pages):
    del pt_ref
    pages = refs[:n_pages]
    iq_ref, iw_ref, new_ref, o_ref = refs[n_pages:]
    iq = iq_ref[...]
    iw = _rounded(iw_ref[...])
    for pg in range(n_pages):
        s = _dot(iq, pages[pg][2 * HD:2 * HD + IDX_DIM, :])
        o_ref[pg:pg + 1, :] = jnp.sum(_rounded(jnp.maximum(s, 0.0)) * iw, axis=0, keepdims=True)
    s_new = jnp.sum(iq.astype(F32) * _rounded(new_ref[:, 2 * HD:2 * HD + IDX_DIM]), axis=-1, keepdims=True)
    sc_new = jnp.sum(_rounded(jnp.maximum(s_new, 0.0)) * iw, axis=0, keepdims=True)
    o_ref[n_pages:n_pages + 1, :] = jnp.broadcast_to(sc_new, (1, PAGE))


def dsa_dec_scores(page_table, cache_dsa, layer, iq, iw, new_rows):
    nb, n_pages = page_table.shape
    gs = pltpu.PrefetchScalarGridSpec(
        num_scalar_prefetch=1, grid=(nb,),
        in_specs=_page_specs((DSA_ROW, PAGE), layer, n_pages) + [
            pl.BlockSpec((None, IDX_H, IDX_DIM), lambda b, pt: (b, 0, 0)),
            pl.BlockSpec((None, IDX_H, 1), lambda b, pt: (b, 0, 0)),
            pl.BlockSpec((None, 1, DSA_ROW), lambda b, pt: (b, 0, 0))],
        out_specs=pl.BlockSpec((None, n_pages + 1, PAGE), lambda b, pt: (b, 0, 0)))
    return pl.pallas_call(
        functools.partial(_dsa_dec_score_kernel, n_pages=n_pages), grid_spec=gs,
        out_shape=jax.ShapeDtypeStruct((nb, n_pages + 1, PAGE), F32),
        compiler_params=_params(("parallel",)),
        name="dsa_dec_scores",
    )(page_table, *([cache_dsa] * n_pages), iq, iw, new_rows)


def _topk_mask_kernel(s_ref, o_ref, *, n_keys, n_top):
    s = s_ref[...]
    valid = lax.broadcasted_iota(I32, s.shape, 1) < n_keys
    key = jnp.where(valid, _sort_key(s), INT_MIN)
    thr = _kth_largest_key(lambda c: jnp.sum(jnp.where(key >= c, 1.0, 0.0), axis=-1, keepdims=True),
                           n_top, (s.shape[0], 1))
    o_ref[...] = jnp.where(valid, jnp.where(key >= thr, 1.0, 0.0), 0.0)


def topk_mask(scores, n_keys, n_top):
    return pl.pallas_call(
        functools.partial(_topk_mask_kernel, n_keys=n_keys, n_top=n_top),
        out_shape=jax.ShapeDtypeStruct(scores.shape, F32),
        compiler_params=pltpu.CompilerParams(vmem_limit_bytes=V7X_VMEM_LIMIT),
        name="topk_mask",
    )(scores)


def _dsa_dec_attend_kernel(pt_ref, *refs, n_pages):
    del pt_ref
    pages = refs[:n_pages]
    q_ref, mask_ref, new_ref, o_ref = refs[n_pages:]
    q = q_ref[...]
    lgs, vs = [], []
    for pg in range(n_pages):
        lg = _dot(_rows8(q), pages[pg][0:HD, :])[0:DSA_H, :] * HD ** -0.5
        lgs.append(jnp.where(mask_ref[pg:pg + 1, :] > 0.5, lg, NEG_INF))
        vs.append(pages[pg][HD:2 * HD, :])
    lg_new = jnp.sum(q.astype(F32) * _rounded(new_ref[:, 0:HD]), axis=-1, keepdims=True) * HD ** -0.5
    lg_new = jnp.where(mask_ref[n_pages:n_pages + 1, 0:1] > 0.5, lg_new, NEG_INF)
    o_ref[...] = _paged_attend(lgs, lg_new, vs, new_ref[:, HD:2 * HD], v_t=True).astype(o_ref.dtype)


def dsa_dec_attend(page_table, cache_dsa, layer, q, mask, new_rows):
    nb, n_pages = page_table.shape
    gs = pltpu.PrefetchScalarGridSpec(
        num_scalar_prefetch=1, grid=(nb,),
        in_specs=_page_specs((DSA_ROW, PAGE), layer, n_pages) + [
            pl.BlockSpec((None, DSA_H, HD), lambda b, pt: (b, 0, 0)),
            pl.BlockSpec((None, n_pages + 1, PAGE), lambda b, pt: (b, 0, 0)),
            pl.BlockSpec((None, 1, DSA_ROW), lambda b, pt: (b, 0, 0))],
        out_specs=pl.BlockSpec((None, DSA_H, HD), lambda b, pt: (b, 0, 0)))
    return pl.pallas_call(
        functools.partial(_dsa_dec_attend_kernel, n_pages=n_pages), grid_spec=gs,
        out_shape=jax.ShapeDtypeStruct((nb, DSA_H, HD), MXU_DTYPE),
        compiler_params=_params(("parallel",)),
        name="dsa_dec_attend",
    )(page_table, *([cache_dsa] * n_pages), q, mask, new_rows)


def _page_head_copy(cache_ref, layer, page, h, buf_ref, slot, pg, sem_ref):
    return pltpu.make_async_copy(cache_ref.at[layer, page, :, h, :], buf_ref.at[slot, pg, h], sem_ref.at[slot])


def _double_buffered_pages(pt_ref, cache_ref, layer, buf_ref, sem_ref, n_pages, n_heads):
    b = pl.program_id(0)
    nb = pl.num_programs(0)
    slot = lax.rem(b, 2)

    def fetch(seq, s):
        for pg in range(n_pages):
            page = pt_ref[seq, pg]
            for h in range(n_heads):
                _page_head_copy(cache_ref, layer, page, h, buf_ref, s, pg, sem_ref).start()

    @pl.when(b == 0)
    def _():
        fetch(b, 0)

    @pl.when(b + 1 < nb)
    def _():
        fetch(b + 1, 1 - slot)

    for pg in range(n_pages):
        for h in range(n_heads):
            _page_head_copy(cache_ref, layer, 0, h, buf_ref, slot, pg, sem_ref).wait()
    return slot


def _moba_dec_kernel(pt_ref, cache_ref, q_ref, new_ref, o_ref, buf_ref, sem_ref, *, layer, n_pages, n_sel):
    slot = _double_buffered_pages(pt_ref, cache_ref, layer, buf_ref, sem_ref, n_pages, MOBA_H)
    ppb = MOBA_BLOCK // PAGE
    nblk = n_pages // ppb
    for h in range(MOBA_H):
        q = q_ref[h:h + 1, :]
        ks = [buf_ref[slot, pg, h, :, 0:HD] for pg in range(n_pages)]
        lane = lax.broadcasted_iota(I32, (1, 128), 1)
        gsc = jnp.full((1, 128), NEG_INF, F32)
        for n in range(nblk):
            tot = jnp.sum(ks[n * ppb], axis=0, keepdims=True)
            for j in range(1, ppb):
                tot = tot + jnp.sum(ks[n * ppb + j], axis=0, keepdims=True)
            mean = tot * (1.0 / MOBA_BLOCK)
            gsc = jnp.where(lane == n, jnp.sum(q * mean, axis=-1, keepdims=True), gsc)
        selm = _top_rank_mask(gsc[:, 0:nblk], n_sel)
        k_new = new_ref[:, h * 2 * HD:h * 2 * HD + HD]
        v_new = new_ref[:, h * 2 * HD + HD:(h + 1) * 2 * HD]
        lg_new = jnp.sum(q * k_new, axis=-1, keepdims=True) * HD ** -0.5
        lgs = []
        m = lg_new
        for pg in range(n_pages):
            lg = jnp.sum(ks[pg] * q, axis=-1, keepdims=True) * HD ** -0.5
            lg = jnp.where(selm[:, pg // ppb:pg // ppb + 1] > 0.5, lg, NEG_INF)
            lgs.append(lg)
            m = jnp.maximum(m, jnp.max(lg, axis=0, keepdims=True))
        p_new = jnp.exp(lg_new - m)
        ps = [jnp.exp(lg - m) for lg in lgs]
        tot = p_new
        for pcol in ps:
            tot = tot + jnp.sum(pcol, axis=0, keepdims=True)
        acc = jnp.zeros((PAGE, HD), F32)
        for pg in range(n_pages):
            acc = acc + _rounded(buf_ref[slot, pg, h, :, HD:2 * HD]) * _rounded(ps[pg] / tot)
        out = jnp.sum(acc, axis=0, keepdims=True) + _rounded(p_new / tot) * _rounded(v_new)
        o_ref[h:h + 1, :] = out.astype(o_ref.dtype)


def moba_dec(page_table, cache_moba, layer, q, new_rows):
    nb, n_pages = page_table.shape
    n_keys = n_pages * PAGE + 1
    n_sel = min(MOBA_TOPK, (n_keys - 1) // MOBA_BLOCK)
    gs = pltpu.PrefetchScalarGridSpec(
        num_scalar_prefetch=1, grid=(nb,),
        in_specs=[pl.BlockSpec(memory_space=pl.ANY),
                  pl.BlockSpec((None, MOBA_H, HD), lambda b, pt: (b, 0, 0)),
                  pl.BlockSpec((None, 1, MOBA_H * 2 * HD), lambda b, pt: (b, 0, 0))],
        out_specs=pl.BlockSpec((None, MOBA_H, HD), lambda b, pt: (b, 0, 0)),
        scratch_shapes=[pltpu.VMEM((2, n_pages, MOBA_H, PAGE, 2 * HD), F32), pltpu.SemaphoreType.DMA((2,))])
    return pl.pallas_call(
        functools.partial(_moba_dec_kernel, layer=layer, n_pages=n_pages, n_sel=n_sel), grid_spec=gs,
        out_shape=jax.ShapeDtypeStruct((nb, MOBA_H, HD), MXU_DTYPE),
        compiler_params=_params(("arbitrary",)),
        name="moba_dec",
    )(page_table, cache_moba, q, new_rows)


def _fox_dec_kernel(pt_ref, cache_ref, q_ref, qt_ref, new_ref, post_ref, ot_ref, buf_ref, sem_ref, *,
                    layer, n_pages):
    slot = _double_buffered_pages(pt_ref, cache_ref, layer, buf_ref, sem_ref, n_pages, FOX_H)
    after = jnp.where(lax.broadcasted_iota(I32, (PAGE, PAGE), 0) > lax.broadcasted_iota(I32, (PAGE, PAGE), 1),
                      1.0, 0.0)
    diag = lax.broadcasted_iota(I32, (HD, HD), 0) == lax.broadcasted_iota(I32, (HD, HD), 1)
    lf_all = jnp.concatenate([buf_ref[slot, pg, h, 2 * HD:2 * HD + 1, :]
                              for pg in range(n_pages) for h in range(FOX_H)], axis=0)
    within_all = _dot_exact(lf_all, after)
    tot_all = jnp.sum(lf_all, axis=-1, keepdims=True)
    for h in range(FOX_H):
        q_col = qt_ref[:, h:h + 1]
        later = post_ref[:, SM_FF + h:SM_FF + h + 1]
        k_new = new_ref[:, h * 2 * HD:h * 2 * HD + HD]
        v_new = new_ref[:, h * 2 * HD + HD:(h + 1) * 2 * HD]
        lg_new = jnp.sum(q_ref[h:h + 1, :] * k_new, axis=-1, keepdims=True) * HD ** -0.5
        lgs = [None] * n_pages
        m = lg_new
        for pg in range(n_pages - 1, -1, -1):
            i = pg * FOX_H + h
            lg = jnp.sum(buf_ref[slot, pg, h, 0:HD, :] * q_col, axis=0, keepdims=True) * HD ** -0.5
            lgs[pg] = lg + (within_all[i:i + 1, :] + later)
            later = later + tot_all[i:i + 1, :]
            m = jnp.maximum(m, jnp.max(lgs[pg], axis=-1, keepdims=True))
        p_new = jnp.exp(lg_new - m)
        ps = [jnp.exp(lg - m) for lg in lgs]
        tot = p_new
        for prow in ps:
            tot = tot + jnp.sum(prow, axis=-1, keepdims=True)
        acc = jnp.zeros((HD, PAGE), F32)
        for pg in range(n_pages):
            acc = acc + buf_ref[slot, pg, h, HD:2 * HD, :] * (ps[pg] / tot)
        v_new_col = jnp.sum(jnp.where(diag, v_new, 0.0), axis=1, keepdims=True)
        ot_ref[:, h:h + 1] = jnp.sum(acc, axis=-1, keepdims=True) + v_new_col * (p_new / tot)


def fox_dec(page_table, cache_fox_t, layer, q, new_kv, post):
    nb, n_pages = page_table.shape
    gs = pltpu.PrefetchScalarGridSpec(
        num_scalar_prefetch=1, grid=(nb,),
        in_specs=[pl.BlockSpec(memory_space=pl.ANY),
                  pl.BlockSpec((None, FOX_H, HD), lambda b, pt: (b, 0, 0)),
                  pl.BlockSpec((None, HD, FOX_H), lambda b, pt: (b, 0, 0)),
                  pl.BlockSpec((None, 1, FOX_H * 2 * HD), lambda b, pt: (b, 0, 0)),
                  pl.BlockSpec((None, 1, 128), lambda b, pt: (b, 0, 0))],
        out_specs=pl.BlockSpec((None, HD, FOX_H), lambda b, pt: (b, 0, 0)),
        scratch_shapes=[pltpu.VMEM((2, n_pages, FOX_H, 2 * HD + 1, PAGE), F32), pltpu.SemaphoreType.DMA((2,))])
    out_t = pl.pallas_call(
        functools.partial(_fox_dec_kernel, layer=layer, n_pages=n_pages), grid_spec=gs,
        out_shape=jax.ShapeDtypeStruct((nb, HD, FOX_H), F32),
        compiler_params=_params(("arbitrary",)),
        name="fox_dec",
    )(page_table, cache_fox_t, q, jnp.transpose(q, (0, 2, 1)), new_kv, post)
    return jnp.transpose(out_t, (0, 2, 1)).astype(MXU_DTYPE)


def _ssd_dec_kernel(st_ref, cs_ref, xbc_ref, z_ref, post_ref, cw_ref, cb_ref, hp_ref, ng_ref,
                    o_ref, sto_ref, cso_ref):
    cs = cs_ref[...]
    xn = xbc_ref[...]
    cw = cw_ref[...]
    conv = cw[0:1, :] * cs[0:1, :]
    for j in range(1, SSM_CONV - 1):
        conv = conv + cw[j:j + 1, :] * cs[j:j + 1, :]
    conv = conv + cw[SSM_CONV - 1:SSM_CONV, :] * xn + cb_ref[...]
    cso_ref[0:SSM_CONV - 2, :] = cs[1:SSM_CONV - 1, :]
    cso_ref[SSM_CONV - 2:SSM_CONV - 1, :] = xn
    u = _silu(conv)
    post = post_ref[...]
    a_row = -jnp.exp(hp_ref[1:2, :])
    diag = lax.broadcasted_iota(I32, (SSM_P, SSM_P), 0) == lax.broadcasted_iota(I32, (SSM_P, SSM_P), 1)
    ys = []
    for h in range(SSM_H):
        g = h // (SSM_H // SSM_G)
        bg = u[:, SSM_DI + g * SSM_N:SSM_DI + (g + 1) * SSM_N]
        cg = u[:, SSM_DI + SSM_G * SSM_N + g * SSM_N:SSM_DI + SSM_G * SSM_N + (g + 1) * SSM_N]
        dt = post[:, SM_DT + h:SM_DT + h + 1]
        adt = dt * a_row[:, SM_DT + h:SM_DT + h + 1]
        x_h = u[:, h * SSM_P:(h + 1) * SSM_P]
        xdt = x_h * dt
        y_diag = xdt * jnp.sum(cg * bg, axis=-1, keepdims=True)
        xdt_col = jnp.sum(jnp.where(diag, xdt, 0.0), axis=1, keepdims=True)
        h0 = _rounded(st_ref[h])
        sto_ref[h] = _rounded(jnp.exp(adt)) * h0 + _rounded(xdt_col * bg)
        y_col = jnp.sum(h0 * _rounded(cg), axis=1, keepdims=True)
        y_off = jnp.exp(adt) * jnp.sum(jnp.where(diag, y_col, 0.0), axis=0, keepdims=True)
        ys.append(y_diag + y_off + x_h * hp_ref[2:3, SM_DT + h:SM_DT + h + 1])
    y = jnp.concatenate(ys, axis=1)
    o_ref[...] = _gated_group_norm(y, z_ref[...], ng_ref[...]).astype(o_ref.dtype)


def ssd_dec(state_ssm, state_conv, layer, xbc, z, post, conv_w, conv_b, head_params, norm_g):
    nb = xbc.shape[0]
    row = lambda w: pl.BlockSpec((None, 1, w), lambda b: (b, 0, 0))
    const = lambda r, w: pl.BlockSpec((r, w), lambda b: (0, 0))
    return pl.pallas_call(
        _ssd_dec_kernel,
        grid=(nb,),
        in_specs=[pl.BlockSpec((None, None, SSM_H, SSM_P, SSM_N), lambda b: (layer, b, 0, 0, 0)),
                  pl.BlockSpec((None, None, SSM_CONV - 1, CONV_DIM), lambda b: (layer, b, 0, 0)),
                  row(CONV_DIM), row(SSM_DI), row(128),
                  const(SSM_CONV, CONV_DIM), const(1, CONV_DIM), const(8, 128), const(1, SSM_DI)],
        out_specs=[row(SSM_DI),
                   pl.BlockSpec((None, SSM_H, SSM_P, SSM_N), lambda b: (b, 0, 0, 0)),
                   pl.BlockSpec((None, SSM_CONV - 1, CONV_DIM), lambda b: (b, 0, 0))],
        out_shape=[jax.ShapeDtypeStruct((nb, 1, SSM_DI), MXU_DTYPE),
                   jax.ShapeDtypeStruct((nb, SSM_H, SSM_P, SSM_N), F32),
                   jax.ShapeDtypeStruct((nb, SSM_CONV - 1, CONV_DIM), F32)],
        compiler_params=_params(("parallel",)),
        name="ssd_dec",
    )(state_ssm, state_conv, xbc.reshape(nb, 1, CONV_DIM), z.reshape(nb, 1, SSM_DI), post.reshape(nb, 1, 128),
      conv_w, conv_b.reshape(1, CONV_DIM), head_params, norm_g.reshape(1, SSM_DI))


def _mem_dec_kernel(q_ref, kv_ref, o_ref):
    for h in range(MEM_H):
        q = q_ref[h:h + 1, :]
        lg = _dot_nt(_rows8(q), kv_ref[:, h, 0:HD])[0:1, :] * HD ** -0.5
        m = jnp.max(lg, axis=-1, keepdims=True)
        p = jnp.exp(lg - m)
        o = _dot(_rows8(p / jnp.sum(p, axis=-1, keepdims=True)), kv_ref[:, h, HD:2 * HD])[0:1, :]
        o_ref[h:h + 1, :] = o.astype(o_ref.dtype)


def mem_dec(mq, cache_mem, layer):
    nb = mq.shape[0]
    mem_len = cache_mem.shape[2]
    return pl.pallas_call(
        _mem_dec_kernel,
        grid=(nb,),
        in_specs=[pl.BlockSpec((None, MEM_H, HD), lambda b: (b, 0, 0)),
                  pl.BlockSpec((None, None, mem_len, MEM_H, 2 * HD), lambda b: (layer, b, 0, 0, 0))],
        out_specs=pl.BlockSpec((None, MEM_H, HD), lambda b: (b, 0, 0)),
        out_shape=jax.ShapeDtypeStruct((nb, MEM_H, HD), MXU_DTYPE),
        compiler_params=_params(("parallel",)),
        name="mem_dec",
    )(mq, cache_mem)


def mixer_sample(xb, lw, layer, p):
    nb = xb.shape[0]
    page_table = p["page_table"]
    n_pages = page_table.shape[1]
    n_keys = n_pages * PAGE + 1
    cache_dsa_t = jnp.transpose(p["cache_dsa"], (0, 1, 3, 2))
    cache_fox_t = jnp.transpose(p["cache_fox"], (0, 1, 4, 3, 2))
    pr = in_proj(xb, lw, F32)
    post, _ = small_post(pr["small"], lw["small_bias"], 1, nb)
    q_all = pr["q_all"]
    q_mxu = q_all[:, :Q_MOBA].astype(MXU_DTYPE)
    dsa_new = pr["dsa_row"].reshape(nb, 1, DSA_ROW)
    scores = dsa_dec_scores(page_table, cache_dsa_t, layer,
                            q_mxu[:, Q_IDX:Q_IDX + IDX_H * IDX_DIM].reshape(nb, IDX_H, IDX_DIM),
                            post[:, SM_IW:SM_IW + IDX_H].reshape(nb, IDX_H, 1), dsa_new)
    mask = topk_mask(scores.reshape(nb, (n_pages + 1) * PAGE), n_keys, min(DSA_TOPK, n_keys // 4))
    o_dsa = dsa_dec_attend(page_table, cache_dsa_t, layer, q_mxu[:, Q_DSA:Q_DSA + DSA_H * HD].reshape(nb, DSA_H, HD),
                           mask.reshape(nb, n_pages + 1, PAGE), dsa_new)
    o_moba = moba_dec(page_table, p["cache_moba"], layer, q_all[:, Q_MOBA:Q_MOBA + MOBA_H * HD].reshape(nb, MOBA_H, HD),
                      pr["moba_row"].reshape(nb, 1, MOBA_H * 2 * HD))
    o_fox = fox_dec(page_table, cache_fox_t, layer, q_all[:, Q_FOX:Q_FOX + FOX_H * HD].reshape(nb, FOX_H, HD),
                    pr["fox_kv"].reshape(nb, 1, FOX_H * 2 * HD), post.reshape(nb, 1, 128))
    o_ssm, ssm_new, conv_new = ssd_dec(p["state_ssm"], p["state_conv"], layer, pr["xbc"], pr["z"], post,
                                       lw["conv_w"], lw["conv_b"], lw["head_params"], lw["norm_g"])
    logf = post[:, SM_FF:SM_FF + FOX_H].reshape(nb, 1, FOX_H, 1)
    rows = (dsa_new,
            pr["moba_row"].reshape(nb, 1, MOBA_H, 2 * HD),
            jnp.concatenate([pr["fox_kv"].reshape(nb, 1, FOX_H, 2 * HD), logf], axis=-1),
            ssm_new, conv_new)
    outs = (o_dsa.reshape(nb, DSA_H * HD), o_moba.reshape(nb, MOBA_H * HD), o_ssm.reshape(nb, SSM_DI),
            o_fox.reshape(nb, FOX_H * HD))
    return outs, rows


def kernel(x_prompt, x_sample, mem_prompt, cache_dsa, cache_moba, cache_fox, cache_mem, state_ssm, state_conv,
           page_table, w_in, fox_b_f, conv_w, conv_b, dt_bias, a_log, d_skip, ssm_norm_g, w_br_dsa, w_br_moba,
           w_br_ssm, w_br_fox, w_gate, b_gate, w_o, ln1_g, ln1_b, w_mq, w_mkv, w_mo, ln2_g, ln2_b, w_router,
           b_router, w1, b1, w2, b2, ln3_g, ln3_b):
    p = dict(cache_dsa=cache_dsa, cache_moba=cache_moba, cache_fox=cache_fox, cache_mem=cache_mem,
             state_ssm=state_ssm, state_conv=state_conv, page_table=page_table, w_in=w_in, fox_b_f=fox_b_f,
             conv_w=conv_w, conv_b=conv_b, dt_bias=dt_bias, a_log=a_log, d_skip=d_skip, ssm_norm_g=ssm_norm_g,
             w_br_dsa=w_br_dsa, w_br_moba=w_br_moba, w_br_ssm=w_br_ssm, w_br_fox=w_br_fox, w_gate=w_gate,
             b_gate=b_gate, w_o=w_o, ln1_g=ln1_g, ln1_b=ln1_b, w_mq=w_mq, w_mkv=w_mkv, w_mo=w_mo, ln2_g=ln2_g,
             ln2_b=ln2_b, w_router=w_router, b_router=b_router, w1=w1, b1=b1, w2=w2, b2=b2, ln3_g=ln3_g,
             ln3_b=ln3_b)
    nb, t, d = x_prompt.shape
    ns = x_sample.shape[0]
    mem_len = mem_prompt.shape[1]
    n_p = nb * t
    xp, xs = x_prompt.reshape(n_p, d), x_sample.reshape(ns, d)
    xpb, xsb = xp.astype(MXU_DTYPE), xs.astype(MXU_DTYPE)
    memb = mem_prompt.reshape(nb * mem_len, d).astype(MXU_DTYPE)
    st_p, st_s = [], []
    for l in range(DEPTH):
        lw = prep_layer(l, p)
        outs_p, rows_p = mixer_prompt(xpb, lw, nb, t)
        outs_s, rows_s = mixer_sample(xsb, lw, l, p)
        h1p, h1pb = mm_ln(gate_merge(xpb, outs_p, lw["w_gate"], lw["b_gate"], lw["projs"]), lw["w_o"], xp, *lw["ln1"])
        h1s, h1sb = mm_ln(gate_merge(xsb, outs_s, lw["w_gate"], lw["b_gate"], lw["projs"]), lw["w_o"], xs, *lw["ln1"])
        mem_kv = mm(memb, lw["w_mkv"])
        om_p = mem_prompt_attn(mm(h1pb, lw["w_mq"], MXU_DTYPE), mem_kv, nb, t, mem_len)
        om_s = mem_dec(mm(h1sb, lw["w_mq"], MXU_DTYPE).reshape(ns, MEM_H, HD), cache_mem, l).reshape(ns, MEM_H * HD)
        h2p, h2pb = mm_ln(om_p, lw["w_mo"], h1p, *lw["ln2"])
        h2s, h2sb = mm_ln(om_s, lw["w_mo"], h1s, *lw["ln2"])
        y, yb = moe_ln(jnp.concatenate([h2p, h2s], axis=0), jnp.concatenate([h2pb, h2sb], axis=0), lw)
        xp, xs, xpb, xsb = y[:n_p], y[n_p:], yb[:n_p], yb[n_p:]
        st_p.append(rows_p + (mem_kv.reshape(nb, mem_len, MEM_H, 2 * HD),))
        st_s.append(rows_s)
    sp = [jnp.stack(z) for z in zip(*st_p)]
    ss = [jnp.stack(z) for z in zip(*st_s)]
    return (xp.reshape(nb, t, d), xs.reshape(ns, 1, d), sp[0], ss[0], sp[1], ss[1], sp[2], ss[2], sp[3], ss[3],
            sp[4], ss[4], sp[5])
```
